```python
import jax, jax.numpy as jnp
from jax import lax
import numpy as np

D_MODEL = 2048
BATCH = 8
SEQ = 2048
DEPTH = 4

D_MIX = D_MODEL
CONV_CH = D_MIX // 4
CONV_GROUPS = 4
CONV_WIDTH = 31
ML_HEADS = 4
ML_DIM = D_MIX // 4
ML_HEAD_DIM = ML_DIM // ML_HEADS
ML_CONV_WIDTH = 4
ML_CHUNK = 128
ATT_DIM = D_MIX - CONV_CH - ML_DIM
ATT_HEADS = 8
ATT_HEAD_DIM = ATT_DIM // ATT_HEADS
MOBA_BLOCK = 256
MOBA_TOPK = 3
MOBA_Q_CHUNK = 16
D_IN = 2 * CONV_CH + 4 * ML_DIM + 2 * ML_HEADS + 3 * ATT_DIM
D_FF = 5632
N_EXPERTS = 8
TOP_K = 2
N_DENSE = (DEPTH + 1) // 2
N_MOE = DEPTH // 2
EPS = 1e-6
NEG = -1e30

kernel_name = 'hymba_style_conv_mlstm_moba_moe_trunk'


def rmsnorm(x, g):
    xf = x.astype(jnp.float32)
    y = xf * lax.rsqrt(jnp.mean(xf * xf, axis=-1, keepdims=True) + EPS)
    return (y * g.astype(jnp.float32)).astype(x.dtype)


def modulate(h, shift, scale):
    return h * (1 + scale[:, None, :]) + shift[:, None, :]


def causal_dwconv(u, w, b):
    k = w.shape[0]
    out = lax.conv_general_dilated(u, w[:, None, :].astype(u.dtype), window_strides=(1,), padding=[(k - 1, 0)], dimension_numbers=('NWC', 'WIO', 'NWC'), feature_group_count=u.shape[-1])
    return out + b.astype(u.dtype)


def conv_module(a, g, w, b, ln_g, ln_b):
    u = causal_dwconv(a * jax.nn.sigmoid(g), w, b)
    B, T, C = u.shape
    uf = u.astype(jnp.float32).reshape(B, T, CONV_GROUPS, C // CONV_GROUPS)
    mu = jnp.mean(uf, axis=-1, keepdims=True)
    var = jnp.mean(jnp.square(uf - mu), axis=-1, keepdims=True)
    uf = ((uf - mu) * lax.rsqrt(var + EPS)).reshape(B, T, C) * ln_g + ln_b
    return jax.nn.silu(uf).astype(a.dtype)


def mlstm_chunk(carry, xs):
    C, n, m = carry
    q, k, v, log_i, log_f = xs
    L = q.shape[2]
    b = jnp.cumsum(log_f, axis=-1)
    causal = jnp.tril(jnp.ones((L, L), dtype=bool))
    D = jnp.where(causal, b[..., :, None] - b[..., None, :] + log_i[..., None, :], NEG)
    inter = b + m[..., None]
    m_t = jnp.maximum(jnp.max(D, axis=-1), inter)
    S = jnp.einsum('bhtd,bhsd->bhts', q, k) * jnp.exp(D - m_t[..., None])
    w_inter = jnp.exp(inter - m_t)
    num = jnp.einsum('bhts,bhsd->bhtd', S, v) + w_inter[..., None] * jnp.einsum('bhvk,bhtk->bhtv', C, q)
    den = jnp.sum(S, axis=-1) + w_inter * jnp.einsum('bhk,bhtk->bht', n, q)
    h = num / jnp.maximum(jnp.abs(den), jnp.exp(-m_t))[..., None]
    b_L = b[..., -1]
    log_w = b_L[..., None] - b + log_i
    m_new = jnp.maximum(b_L + m, jnp.max(log_w, axis=-1))
    wk = jnp.exp(log_w - m_new[..., None])
    decay = jnp.exp(b_L + m - m_new)
    C_new = decay[..., None, None] * C + jnp.einsum('bhs,bhsv,bhsk->bhvk', wk, v, k)
    n_new = decay[..., None] * n + jnp.einsum('bhs,bhsk->bhk', wk, k)
    return (C_new, n_new, m_new), h


def mlstm(q, k, v, i_pre, f_pre, b_i, b_f, norm_g):
    B, T, _ = q.shape
    H, dh, L = ML_HEADS, ML_HEAD_DIM, ML_CHUNK
    NC = T // L
    def heads(t):
        return t.astype(jnp.float32).reshape(B, T, H, dh).transpose(0, 2, 1, 3).reshape(B, H, NC, L, dh).transpose(2, 0, 1, 3, 4)
    def gates(t):
        return t.astype(jnp.float32).transpose(0, 2, 1).reshape(B, H, NC, L).transpose(2, 0, 1, 3)
    log_i = gates(i_pre + b_i)
    log_f = jax.nn.log_sigmoid(gates(f_pre + b_f))
    init = (jnp.zeros((B, H, dh, dh), jnp.float32), jnp.zeros((B, H, dh), jnp.float32), jnp.zeros((B, H), jnp.float32))
    _, hs = lax.scan(mlstm_chunk, init, (heads(q), heads(k) * (dh ** -0.5), heads(v), log_i, log_f))
    hs = hs.transpose(1, 2, 0, 3, 4).reshape(B, H, T, dh).transpose(0, 2, 1, 3)
    hs = hs * lax.rsqrt(jnp.mean(hs * hs, axis=-1, keepdims=True) + EPS) * norm_g.astype(jnp.float32).reshape(H, dh)
    return hs.reshape(B, T, H * dh)


def moba_attention(q, k, v, norm_g):
    B, T, _ = q.shape
    H, dh, BLK, QC = ATT_HEADS, ATT_HEAD_DIM, MOBA_BLOCK, MOBA_Q_CHUNK
    NB = -(-T // BLK)
    T_pad = NB * BLK
    topk = min(MOBA_TOPK, NB)
    NSEL = topk + 1
    def heads(t):
        t = t.reshape(B, T, H, dh).transpose(0, 2, 1, 3)
        return jnp.pad(t, ((0, 0), (0, 0), (0, T_pad - T), (0, 0)))
    qh = heads(q) * (dh ** -0.5)
    kb = heads(k).reshape(B, H, NB, BLK, dh)
    vb = heads(v).reshape(B, H, NB, BLK, dh)
    pos = jnp.arange(T_pad)
    qblk = pos // BLK
    k_mean = jnp.mean(kb.astype(jnp.float32), axis=3)
    gate = jnp.einsum('bhtd,bhnd->bhtn', qh.astype(jnp.float32), k_mean)
    past = jnp.arange(NB)[None, :] < qblk[:, None]
    gate = jnp.where(past, gate, NEG)
    _, top_idx = lax.top_k(gate, topk)
    own = jnp.broadcast_to(qblk[None, None, :, None], (B, H, T_pad, 1)).astype(top_idx.dtype)
    sel = jnp.concatenate([top_idx, own], axis=-1)
    nq = T_pad // QC
    q_c = qh.reshape(B, H, nq, QC, dh).transpose(2, 0, 1, 3, 4)
    sel_c = sel.reshape(B, H, nq, QC, NSEL).transpose(2, 0, 1, 3, 4)
    pos_c = pos.reshape(nq, QC)
    slopes = 2.0 ** (-(8.0 / H) * jnp.arange(1, H + 1, dtype=jnp.float32))
    slot_routed = jnp.arange(NSEL) < topk
    gather = jax.vmap(jax.vmap(lambda table, ix: table[ix]))
    def attend(args):
        qc, ic, pc = args
        kg = gather(kb, ic)
        vg = gather(vb, ic)
        logits = jnp.einsum('bhqd,bhqnsd->bhqns', qc, kg).astype(jnp.float32)
        kpos = ic[..., None] * BLK + jnp.arange(BLK)
        dist = pc[:, None, None] - kpos
        logits = logits - slopes[:, None, None, None] * jnp.abs(dist).astype(jnp.float32)
        routed_ok = slot_routed & (ic < (pc // BLK)[:, None])
        valid = routed_ok[..., None] | ((~slot_routed)[:, None] & (dist >= 0))
        logits = jnp.where(valid, logits, NEG)
        p = jax.nn.softmax(logits.reshape(B, H, QC, NSEL * BLK), axis=-1).reshape(logits.shape)
        return jnp.einsum('bhqns,bhqnsd->bhqd', p.astype(vg.dtype), vg)
    out = lax.map(attend, (q_c, sel_c, pos_c))
    out = out.transpose(1, 2, 0, 3, 4).reshape(B, H, T_pad, dh)[:, :, :T].transpose(0, 2, 1, 3)
    of = out.astype(jnp.float32)
    of = of * lax.rsqrt(jnp.mean(of * of, axis=-1, keepdims=True) + EPS) * norm_g.astype(jnp.float32).reshape(H, dh)
    return of.reshape(B, T, H * dh).astype(q.dtype)


def hybrid_mixer(h, w_in, conv_w, conv_b, conv_ln_g, conv_ln_b, ml_conv_w, ml_conv_b, ml_b_i, ml_b_f, ml_norm_g, attn_norm_g, w_out):
    z = h @ w_in
    sizes = [CONV_CH, CONV_CH, ML_DIM, ML_DIM, ML_DIM, ML_DIM, ML_HEADS, ML_HEADS, ATT_DIM, ATT_DIM, ATT_DIM]
    ca, cg, mq, mk, mv, mo, mi, mf, aq, ak, av = jnp.split(z, [int(s) for s in np.cumsum(sizes)[:-1]], axis=-1)
    y_conv = conv_module(ca, cg, conv_w, conv_b, conv_ln_g, conv_ln_b)
    qk = jax.nn.silu(causal_dwconv(jnp.concatenate([mq, mk], axis=-1), ml_conv_w, ml_conv_b))
    mq, mk = jnp.split(qk, 2, axis=-1)
    y_ml = (jax.nn.sigmoid(mo.astype(jnp.float32)) * mlstm(mq, mk, mv, mi, mf, ml_b_i, ml_b_f, ml_norm_g)).astype(h.dtype)
    y_att = moba_attention(aq, ak, av, attn_norm_g)
    return jnp.concatenate([y_conv, y_ml, y_att], axis=-1) @ w_out


def swiglu(h, wg, wu, wd):
    return (jax.nn.silu(h @ wg) * (h @ wu)) @ wd


def moe_swiglu(h, w_router, b_router, w_gate, w_up, w_down):
    B, T, D = h.shape
    hf = h.reshape(B * T, D)
    logits = (hf @ w_router + b_router).astype(jnp.float32)
    top_val, top_idx = lax.top_k(logits, TOP_K)
    probs = jax.nn.softmax(top_val, axis=-1)
    gates = jnp.sum(jax.nn.one_hot(top_idx, N_EXPERTS, dtype=jnp.float32) * probs[..., None], axis=1).astype(hf.dtype)
    y = jnp.zeros_like(hf)
    for e in range(N_EXPERTS):
        y = y + gates[:, e:e + 1] * swiglu(hf, w_gate[e], w_up[e], w_down[e])
    return y.reshape(B, T, D)


def setup_inputs(seed: int = 0) -> dict:
    key = jax.random.key(seed)
    ks = jax.random.split(key, 32)
    f32 = jnp.float32
    D = D_MODEL
    def nrm(k, shape, scale):
        return jax.random.normal(k, shape, f32) * scale
    return {
        'x': nrm(ks[0], (BATCH, SEQ, D), 1.0),
        'c': nrm(ks[1], (BATCH, D), 1.0),
        'w_mod': nrm(ks[2], (DEPTH, D, 6 * D), 0.5 * D ** -0.5),
        'b_mod': nrm(ks[3], (DEPTH, 6 * D), 0.01),
        'g_mix': 1.0 + nrm(ks[4], (DEPTH, D), 0.02),
        'g_ffn': 1.0 + nrm(ks[5], (DEPTH, D), 0.02),
        'w_in': nrm(ks[6], (DEPTH, D, D_IN), D ** -0.5),
        'conv_w': nrm(ks[7], (DEPTH, CONV_WIDTH, CONV_CH), CONV_WIDTH ** -0.5),
        'conv_b': nrm(ks[8], (DEPTH, CONV_CH), 0.01),
        'conv_ln_g': 1.0 + nrm(ks[9], (DEPTH, CONV_CH), 0.02),
        'conv_ln_b': nrm(ks[10], (DEPTH, CONV_CH), 0.01),
        'ml_conv_w': nrm(ks[11], (DEPTH, ML_CONV_WIDTH, 2 * ML_DIM), ML_CONV_WIDTH ** -0.5),
        'ml_conv_b': nrm(ks[12], (DEPTH, 2 * ML_DIM), 0.01),
        'ml_b_i': nrm(ks[13], (DEPTH, ML_HEADS), 0.1),
        'ml_b_f': jnp.linspace(3.0, 6.0, ML_HEADS, dtype=f32)[None, :] + nrm(ks[14], (DEPTH, ML_HEADS), 0.1),
        'ml_norm_g': 1.0 + nrm(ks[15], (DEPTH, ML_DIM), 0.02),
        'attn_norm_g': 1.0 + nrm(ks[16], (DEPTH, ATT_DIM), 0.02),
        'w_out': nrm(ks[17], (DEPTH, D_MIX, D), D_MIX ** -0.5),
        'ffn_w_gate': nrm(ks[18], (N_DENSE, D, D_FF), D ** -0.5),
        'ffn_w_up': nrm(ks[19], (N_DENSE, D, D_FF), D ** -0.5),
        'ffn_w_down': nrm(ks[20], (N_DENSE, D_FF, D), D_FF ** -0.5),
        'moe_w_router': nrm(ks[21], (N_MOE, D, N_EXPERTS), D ** -0.5),
        'moe_b_router': nrm(ks[22], (N_MOE, N_EXPERTS), 0.01),
        'moe_w_gate': nrm(ks[23], (N_MOE, N_EXPERTS, D, D_FF), D ** -0.5),
        'moe_w_up': nrm(ks[24], (N_MOE, N_EXPERTS, D, D_FF), D ** -0.5),
        'moe_w_down': nrm(ks[25], (N_MOE, N_EXPERTS, D_FF, D), D_FF ** -0.5),
        'g_final': 1.0 + nrm(ks[26], (D,), 0.02),
    }


def reference(x, c, w_mod, b_mod, g_mix, g_ffn, w_in, conv_w, conv_b, conv_ln_g, conv_ln_b, ml_conv_w, ml_conv_b, ml_b_i, ml_b_f, ml_norm_g, attn_norm_g, w_out, ffn_w_gate, ffn_w_up, ffn_w_down, moe_w_router, moe_b_router, moe_w_gate, moe_w_up, moe_w_down, g_final):
    cond = jax.nn.silu(c)
    for l in range(DEPTH):
        mod = cond @ w_mod[l] + b_mod[l]
        sh1, sc1, gt1, sh2, sc2, gt2 = jnp.split(mod, 6, axis=-1)
        h = modulate(rmsnorm(x, g_mix[l]), sh1, sc1)
        y = hybrid_mixer(h, w_in[l], conv_w[l], conv_b[l], conv_ln_g[l], conv_ln_b[l], ml_conv_w[l], ml_conv_b[l], ml_b_i[l], ml_b_f[l], ml_norm_g[l], attn_norm_g[l], w_out[l])
        x = x + gt1[:, None, :] * y
        h = modulate(rmsnorm(x, g_ffn[l]), sh2, sc2)
        if l % 2 == 0:
            f = swiglu(h, ffn_w_gate[l // 2], ffn_w_up[l // 2], ffn_w_down[l // 2])
        else:
            f = moe_swiglu(h, moe_w_router[l // 2], moe_b_router[l // 2], moe_w_gate[l // 2], moe_w_up[l // 2], moe_w_down[l // 2])
        x = x + gt2[:, None, :] * f
    return rmsnorm(x, g_final)
```

```python
import functools

import jax
import jax.numpy as jnp
from jax import lax
from jax.experimental import pallas as pl
from jax.experimental.pallas import tpu as pltpu

f32 = jnp.float32
bf16 = jnp.bfloat16
HIGHEST = lax.Precision.HIGHEST

CONV_GROUPS = 4
HEAD_DIM = 128
ML_CHUNK = 128
MOBA_BLOCK = 256
MOBA_TOPK = 3
TOP_K = 2
EPS = 1e-6
NEG = -1e30

VMEM_LIMIT_BYTES = 56 * 1024 * 1024
LANE = 128


def _cparams(*sem):
    return pltpu.CompilerParams(dimension_semantics=sem, vmem_limit_bytes=VMEM_LIMIT_BYTES)


def _sigmoid(v):
    return 1.0 / (1.0 + jnp.exp(-v))


def _silu(v):
    return v * _sigmoid(v)


def _dot_nt(a, b, **kw):
    return lax.dot_general(a, b, (((1,), (1,)), ((), ())), preferred_element_type=f32, **kw)


def _dot_tn(a, b, **kw):
    return lax.dot_general(a, b, (((0,), (0,)), ((), ())), preferred_element_type=f32, **kw)


def _mod_kernel(c_ref, w_ref, b_ref, o_ref):
    cond = _silu(c_ref[...]).astype(bf16)
    o_ref[...] = jnp.dot(cond, w_ref[...].astype(bf16), preferred_element_type=f32) + b_ref[...]


def _adaln_mod(c, w_mod, b_mod, tn=1024):
    depth, d, n6 = w_mod.shape
    b = c.shape[0]
    return pl.pallas_call(
        _mod_kernel,
        grid=(depth, n6 // tn),
        in_specs=[
            pl.BlockSpec((b, d), lambda l, n: (0, 0)),
            pl.BlockSpec((None, d, tn), lambda l, n: (l, 0, n)),
            pl.BlockSpec((None, 1, tn), lambda l, n: (l, 0, n)),
        ],
        out_specs=pl.BlockSpec((None, b, tn), lambda l, n: (l, 0, n)),
        out_shape=jax.ShapeDtypeStruct((depth, b, n6), f32),
        compiler_params=_cparams("parallel", "parallel"),
        name="adaln_mod",
    )(c, w_mod, b_mod.reshape(depth, 1, n6))


def _norm_modulate(x, g, sh, sc):
    ms = jnp.mean(x * x, axis=-1, keepdims=True)
    h = x * lax.rsqrt(ms + EPS) * g
    return h * (1.0 + sc) + sh


def _inproj_kernel(x_ref, g_ref, sh_ref, sc_ref, w_ref, wg_ref, z_ref, zg_ref, h_ref):
    @pl.when(pl.program_id(2) == 0)
    def _():
        h = _norm_modulate(x_ref[...], g_ref[...], sh_ref[...], sc_ref[...])
        h_ref[...] = h.astype(bf16)
        zg_ref[...] = jnp.dot(h, wg_ref[...], precision=HIGHEST, preferred_element_type=f32)

    z_ref[...] = jnp.dot(h_ref[...], w_ref[...], preferred_element_type=f32).astype(bf16)


def _in_projection(x, g, mod4, layer, w_main, w_gate, tm=512, tn=1024):
    b, t, d = x.shape
    n = w_main.shape[1]
    ng = w_gate.shape[1]
    return pl.pallas_call(
        _inproj_kernel,
        grid=(b, t // tm, n // tn),
        in_specs=[
            pl.BlockSpec((None, tm, d), lambda bi, i, j: (bi, i, 0)),
            pl.BlockSpec((1, d), lambda bi, i, j: (0, 0)),
            pl.BlockSpec((None, None, 1, d), lambda bi, i, j: (layer, bi, 0, 0)),
            pl.BlockSpec((None, None, 1, d), lambda bi, i, j: (layer, bi, 0, 1)),
            pl.BlockSpec((d, tn), lambda bi, i, j: (0, j)),
            pl.BlockSpec((d, ng), lambda bi, i, j: (0, 0)),
        ],
        out_specs=[
            pl.BlockSpec((None, tm, tn), lambda bi, i, j: (bi, i, j)),
            pl.BlockSpec((None, tm, ng), lambda bi, i, j: (bi, i, 0)),
        ],
        out_shape=[
            jax.ShapeDtypeStruct((b, t, n), bf16),
            jax.ShapeDtypeStruct((b, t, ng), f32),
        ],
        scratch_shapes=[pltpu.VMEM((tm, d), bf16)],
        compiler_params=_cparams("parallel", "parallel", "arbitrary"),
        name="in_projection",
    )(x, g.reshape(1, d), mod4, mod4, w_main, w_gate)


CONV_PAD = 32
ROWS = 128


def _conv_kernel(ag_ref, w_ref, b_ref, lg_ref, lb_ref, o_ref, xp_ref):
    t, c = o_ref.shape
    taps = w_ref.shape[0]
    gc = c // CONV_GROUPS
    for cg in range(CONV_GROUPS):
        cs = slice(cg * gc, (cg + 1) * gc)
        gs = slice(c + cg * gc, c + (cg + 1) * gc)
        xp_ref[cg, 0:CONV_PAD, :] = jnp.zeros((CONV_PAD, gc), f32)

        def glu(i, _, cg=cg, cs=cs, gs=gs):
            r0 = pl.multiple_of(i * ROWS, ROWS)
            a = ag_ref[pl.ds(r0, ROWS), cs].astype(f32)
            g = ag_ref[pl.ds(r0, ROWS), gs].astype(f32)
            xp_ref[cg, pl.ds(CONV_PAD + r0, ROWS), :] = a * _sigmoid(g)
            return 0

        lax.fori_loop(0, t // ROWS, glu, 0)

        def body(i, _, cg=cg, cs=cs):
            r0 = pl.multiple_of(i * ROWS, ROWS)
            acc = jnp.zeros((ROWS, gc), f32) + b_ref[:, cs]
            for k in range(taps):
                acc = acc + xp_ref[cg, pl.ds(r0 + (CONV_PAD - (taps - 1) + k), ROWS), :] * w_ref[k:k + 1, cs]
            mu = jnp.mean(acc, axis=-1, keepdims=True)
            dv = acc - mu
            var = jnp.mean(dv * dv, axis=-1, keepdims=True)
            y = dv * lax.rsqrt(var + EPS) * lg_ref[:, cs] + lb_ref[:, cs]
            o_ref[pl.ds(r0, ROWS), cs] = _silu(y).astype(o_ref.dtype)
            return 0

        lax.fori_loop(0, t // ROWS, body, 0)


def _conv_module(z, conv_w, conv_b, ln_g, ln_b):
    b, t, _ = z.shape
    taps, c = conv_w.shape
    assert taps - 1 <= CONV_PAD and c // CONV_GROUPS == LANE
    return pl.pallas_call(
        _conv_kernel,
        grid=(b,),
        in_specs=[
            pl.BlockSpec((None, t, 2 * c), lambda bi: (bi, 0, 0)),
            pl.BlockSpec((taps, c), lambda bi: (0, 0)),
            pl.BlockSpec((1, c), lambda bi: (0, 0)),
            pl.BlockSpec((1, c), lambda bi: (0, 0)),
            pl.BlockSpec((1, c), lambda bi: (0, 0)),
        ],
        out_specs=pl.BlockSpec((None, t, c), lambda bi: (bi, 0, 0)),
        out_shape=jax.ShapeDtypeStruct((b, t, c), bf16),
        scratch_shapes=[pltpu.VMEM((CONV_GROUPS, t + CONV_PAD, c // CONV_GROUPS), f32)],
        compiler_params=_cparams("parallel"),
        name="conv_module",
    )(z, conv_w, conv_b.reshape(1, c), ln_g.reshape(1, c), ln_b.reshape(1, c))


ML_PAD = 8


def _mlstm_kernel(q_ref, k_ref, v_ref, o_ref, gc_ref, gr_ref, cw_ref, cb_ref, bi_ref, bf_ref, bfc_ref,
                  ng_ref, y_ref, qp_ref, kp_ref, qs_ref, ks_ref, bcol_ref, brow_ref):
    t, dm = q_ref.shape
    nh = dm // HEAD_DIM
    L = ML_CHUNK
    nc = t // L
    taps = cw_ref.shape[0]
    kscale = HEAD_DIM ** -0.5

    for h in range(nh):
        hs = slice(h * HEAD_DIM, (h + 1) * HEAD_DIM)
        ks_cols = slice(dm + h * HEAD_DIM, dm + (h + 1) * HEAD_DIM)
        qp_ref[h, 0:ML_PAD, :] = jnp.zeros((ML_PAD, HEAD_DIM), f32)
        kp_ref[h, 0:ML_PAD, :] = jnp.zeros((ML_PAD, HEAD_DIM), f32)

        def stage(i, _, h=h, hs=hs):
            r0 = pl.multiple_of(i * ROWS, ROWS)
            qp_ref[h, pl.ds(ML_PAD + r0, ROWS), :] = q_ref[pl.ds(r0, ROWS), hs].astype(f32)
            kp_ref[h, pl.ds(ML_PAD + r0, ROWS), :] = k_ref[pl.ds(r0, ROWS), hs].astype(f32)
            return 0

        lax.fori_loop(0, t // ROWS, stage, 0)

        def sconv(i, _, h=h, hs=hs, ks_cols=ks_cols):
            r0 = pl.multiple_of(i * ROWS, ROWS)
            aq = jnp.zeros((ROWS, HEAD_DIM), f32) + cb_ref[:, hs]
            ak = jnp.zeros((ROWS, HEAD_DIM), f32) + cb_ref[:, ks_cols]
            for j in range(taps):
                off = ML_PAD - (taps - 1) + j
                aq = aq + qp_ref[h, pl.ds(r0 + off, ROWS), :] * cw_ref[j:j + 1, hs]
                ak = ak + kp_ref[h, pl.ds(r0 + off, ROWS), :] * cw_ref[j:j + 1, ks_cols]
            qs_ref[pl.ds(r0, ROWS), hs] = _silu(aq).astype(bf16)
            ks_ref[pl.ds(r0, ROWS), hs] = (_silu(ak) * kscale).astype(bf16)
            return 0

        lax.fori_loop(0, t // ROWS, sconv, 0)

    ri = lax.broadcasted_iota(jnp.int32, (L, L), 0)
    ci = lax.broadcasted_iota(jnp.int32, (L, L), 1)
    causal = ri >= ci
    lower = causal.astype(f32)
    upper = (ri <= ci).astype(f32)

    def log_sigmoid(v):
        return jnp.minimum(v, 0.0) - jnp.log(1.0 + jnp.exp(-jnp.abs(v)))

    def cums(i, _):
        r0 = pl.multiple_of(i * L, L)
        lf_col = log_sigmoid(gc_ref[pl.ds(r0, L), nh:2 * nh] + bf_ref[...])
        bcol_ref[pl.ds(r0, L), :] = jnp.dot(lower, lf_col, precision=HIGHEST, preferred_element_type=f32)
        lf_row = log_sigmoid(gr_ref[nh:2 * nh, pl.ds(r0, L)] + bfc_ref[...])
        brow_ref[:, pl.ds(r0, L)] = jnp.dot(lf_row, upper, precision=HIGHEST, preferred_element_type=f32)
        return 0

    lax.fori_loop(0, nc, cums, 0)

    for h in range(nh):
        hs = slice(h * HEAD_DIM, (h + 1) * HEAD_DIM)

        def chunk(i, carry, h=h, hs=hs):
            ct, n, m = carry
            r0 = pl.multiple_of(i * L, L)
            q = qs_ref[pl.ds(r0, L), hs]
            k = ks_ref[pl.ds(r0, L), hs]
            v = v_ref[pl.ds(r0, L), hs]
            b_col = bcol_ref[pl.ds(r0, L), h:h + 1]
            b_row = brow_ref[h:h + 1, pl.ds(r0, L)]
            li_col = gc_ref[pl.ds(r0, L), h:h + 1] + bi_ref[:, h:h + 1]
            li_row = gr_ref[h:h + 1, pl.ds(r0, L)] + bi_ref[:, h:h + 1]

            dmat = jnp.where(causal, b_col - b_row + li_row, NEG)
            inter = b_col + m
            m_t = jnp.maximum(jnp.max(dmat, axis=-1, keepdims=True), inter)
            s = _dot_nt(q, k) * jnp.exp(dmat - m_t)
            w_inter = jnp.exp(inter - m_t)
            qf = q.astype(f32)
            num = jnp.dot(s.astype(bf16), v, preferred_element_type=f32) \
                + w_inter * jnp.dot(q, ct.astype(bf16), preferred_element_type=f32)
            den = jnp.sum(s + (w_inter * qf) * n, axis=-1, keepdims=True)
            hout = num / jnp.maximum(jnp.abs(den), jnp.exp(-m_t))

            b_last = b_row[:, L - 1:L]
            lw_row = b_last - b_row + li_row
            m_new = jnp.maximum(b_last + m, jnp.max(lw_row, axis=-1, keepdims=True))
            wk_col = jnp.exp(b_last - b_col + li_col - m_new)
            decay = jnp.exp(b_last + m - m_new)
            kf = k.astype(f32)
            ct_new = decay * ct + _dot_tn(k, (wk_col * v.astype(f32)).astype(bf16))
            n_new = decay * n + jnp.sum(wk_col * kf, axis=0, keepdims=True)

            hn = hout * lax.rsqrt(jnp.mean(hout * hout, axis=-1, keepdims=True) + EPS) * ng_ref[:, hs]
            og = _sigmoid(o_ref[pl.ds(r0, L), hs].astype(f32))
            y_ref[pl.ds(r0, L), hs] = (og * hn).astype(y_ref.dtype)
            return ct_new, n_new, m_new

        init = (jnp.zeros((HEAD_DIM, HEAD_DIM), f32), jnp.zeros((1, HEAD_DIM), f32), jnp.zeros((1, 1), f32))
        lax.fori_loop(0, nc, chunk, init)


def _mlstm(z, zg, zg_t, col0, ml_conv_w, ml_conv_b, b_i, b_f, norm_g):
    b, t, _ = z.shape
    dm = norm_g.shape[0]
    nh = dm // HEAD_DIM
    taps = ml_conv_w.shape[0]
    assert taps - 1 <= ML_PAD

    def zspec(j):
        return pl.BlockSpec((None, t, dm), lambda bi: (bi, 0, col0 + j))

    return pl.pallas_call(
        _mlstm_kernel,
        grid=(b,),
        in_specs=[
            zspec(0), zspec(1), zspec(2), zspec(3),
            pl.BlockSpec((None, t, 2 * nh), lambda bi: (bi, 0, 0)),
            pl.BlockSpec((None, 2 * nh, t), lambda bi: (bi, 0, 0)),
            pl.BlockSpec((taps, 2 * dm), lambda bi: (0, 0)),
            pl.BlockSpec((1, 2 * dm), lambda bi: (0, 0)),
            pl.BlockSpec((1, nh), lambda bi: (0, 0)),
            pl.BlockSpec((1, nh), lambda bi: (0, 0)),
            pl.BlockSpec((nh, 1), lambda bi: (0, 0)),
            pl.BlockSpec((1, dm), lambda bi: (0, 0)),
        ],
        out_specs=pl.BlockSpec((None, t, dm), lambda bi: (bi, 0, 0)),
        out_shape=jax.ShapeDtypeStruct((b, t, dm), bf16),
        scratch_shapes=[
            pltpu.VMEM((nh, t + ML_PAD, HEAD_DIM), f32),
            pltpu.VMEM((nh, t + ML_PAD, HEAD_DIM), f32),
            pltpu.VMEM((t, dm), bf16),
            pltpu.VMEM((t, dm), bf16),
            pltpu.VMEM((t, nh), f32),
            pltpu.VMEM((nh, t), f32),
        ],
        compiler_params=_cparams("parallel"),
        name="mlstm",
    )(z, z, z, z, zg, zg_t, ml_conv_w, ml_conv_b.reshape(1, 2 * dm), b_i.reshape(1, nh),
      b_f.reshape(1, nh), b_f.reshape(nh, 1), norm_g.reshape(1, dm))


def _moba_kernel(q_ref, k_ref, v_ref, al_ref, ng_ref, y_ref, km_ref):
    t, dh = q_ref.shape
    blk = MOBA_BLOCK
    nb = t // blk
    scale = dh ** -0.5

    for n in range(nb):
        km_ref[n:n + 1, :] = jnp.mean(k_ref[n * blk:(n + 1) * blk, :].astype(f32), axis=0, keepdims=True)
    gate_t = _dot_nt(km_ref[...], q_ref[...].astype(f32), precision=HIGHEST) * scale

    blk_id = lax.broadcasted_iota(jnp.int32, (nb, blk), 0)
    ri = lax.broadcasted_iota(jnp.int32, (blk, blk), 0)
    ci = lax.broadcasted_iota(jnp.int32, (blk, blk), 1)
    causal = ri >= ci
    eye = (ri == ci).astype(bf16)
    alibi = al_ref[...]
    slope_blk = al_ref[1:2, 0:1] * float(blk)

    for j in range(nb):
        qj = q_ref[j * blk:(j + 1) * blk, :]
        if j > 0:
            g = jnp.where(blk_id < j, gate_t[:, j * blk:(j + 1) * blk], NEG)
            sel_t = jnp.zeros((nb, blk), f32)
            for n in range(j):
                gn = g[n:n + 1, :]
                ahead = (g > gn) | ((g == gn) & (blk_id < n))
                rank = jnp.sum(jnp.where(ahead, 1.0, 0.0), axis=0, keepdims=True)
                sel_t = jnp.where((blk_id == n) & (rank < float(MOBA_TOPK)), 1.0, sel_t)
            sel = _dot_nt(eye, sel_t.astype(bf16))

        m = jnp.full((blk, 1), NEG, f32)
        l = jnp.zeros((blk, 1), f32)
        acc = jnp.zeros((blk, dh), f32)
        for n in range(j + 1):
            kn = k_ref[n * blk:(n + 1) * blk, :]
            vn = v_ref[n * blk:(n + 1) * blk, :]
            s = _dot_nt(qj, kn) * scale - alibi
            if n < j:
                s = jnp.where(sel[:, n:n + 1] > 0.5, s, NEG)
            else:
                s = jnp.where(causal, s, NEG)
            off = slope_blk * float(j - n)
            m_new = jnp.maximum(m, jnp.max(s, axis=-1, keepdims=True) - off)
            p = jnp.exp(s - (m_new + off))
            alpha = jnp.exp(m - m_new)
            l = alpha * l + jnp.sum(p, axis=-1, keepdims=True)
            acc = alpha * acc + jnp.dot(p.astype(bf16), vn, preferred_element_type=f32)
            m = m_new
        o = acc / l
        o = o * lax.rsqrt(jnp.mean(o * o, axis=-1, keepdims=True) + EPS) * ng_ref[...]
        y_ref[j * blk:(j + 1) * blk, :] = o.astype(y_ref.dtype)


def _moba(z, col0, norm_g):
    b, t, _ = z.shape
    dh = HEAD_DIM
    nh = norm_g.shape[0] // dh
    blk = MOBA_BLOCK
    assert t % blk == 0
    slopes = 2.0 ** (-(8.0 / nh) * jnp.arange(1, nh + 1, dtype=f32))
    rel = (jnp.arange(blk, dtype=f32)[:, None] - jnp.arange(blk, dtype=f32)[None, :])
    alibi = slopes[:, None, None] * rel[None]

    def zspec(j):
        return pl.BlockSpec((None, t, dh), lambda bi, h: (bi, 0, col0 + j * nh + h))

    return pl.pallas_call(
        _moba_kernel,
        grid=(b, nh),
        in_specs=[
            zspec(0), zspec(1), zspec(2),
            pl.BlockSpec((None, blk, blk), lambda bi, h: (h, 0, 0)),
            pl.BlockSpec((None, 1, dh), lambda bi, h: (h, 0, 0)),
        ],
        out_specs=pl.BlockSpec((None, t, dh), lambda bi, h: (bi, 0, h)),
        out_shape=jax.ShapeDtypeStruct((b, t, nh * dh), bf16),
        scratch_shapes=[pltpu.VMEM((t // blk, dh), f32)],
        compiler_params=_cparams("parallel", "parallel"),
        name="moba_attention",
    )(z, z, z, alibi, norm_g.reshape(nh, 1, dh))


def _outproj_kernel(yc_ref, ym_ref, ya_ref, wc_ref, wm_ref, wa_ref, x_ref, gt_ref, o_ref):
    acc = jnp.dot(yc_ref[...], wc_ref[...], preferred_element_type=f32)
    acc = acc + jnp.dot(ym_ref[...], wm_ref[...], preferred_element_type=f32)
    acc = acc + jnp.dot(ya_ref[...], wa_ref[...], preferred_element_type=f32)
    o_ref[...] = x_ref[...] + gt_ref[...] * acc


def _out_projection(y_conv, y_ml, y_att, w_out, x, mod4, layer, tm=1024, tn=1024):
    b, t, d = x.shape
    c1, c2, c3 = y_conv.shape[-1], y_ml.shape[-1], y_att.shape[-1]
    assert c1 == c2 and c3 == c1 + c2
    return pl.pallas_call(
        _outproj_kernel,
        grid=(b, t // tm, d // tn),
        in_specs=[
            pl.BlockSpec((None, tm, c1), lambda bi, i, j: (bi, i, 0)),
            pl.BlockSpec((None, tm, c2), lambda bi, i, j: (bi, i, 0)),
            pl.BlockSpec((None, tm, c3), lambda bi, i, j: (bi, i, 0)),
            pl.BlockSpec((c1, tn), lambda bi, i, j: (0, j)),
            pl.BlockSpec((c2, tn), lambda bi, i, j: (1, j)),
            pl.BlockSpec((c3, tn), lambda bi, i, j: (1, j)),
            pl.BlockSpec((None, tm, tn), lambda bi, i, j: (bi, i, j)),
            pl.BlockSpec((None, None, 1, tn), lambda bi, i, j: (layer, bi, 0, 2 * (d // tn) + j)),
        ],
        out_specs=pl.BlockSpec((None, tm, tn), lambda bi, i, j: (bi, i, j)),
        out_shape=jax.ShapeDtypeStruct((b, t, d), f32),
        compiler_params=_cparams("parallel", "parallel", "parallel"),
        name="out_projection",
    )(y_conv, y_ml, y_att, w_out, w_out, w_out, x, mod4)


def _ffn_prep_kernel(x_ref, g_ref, sh_ref, sc_ref, h_ref):
    h_ref[...] = _norm_modulate(x_ref[...], g_ref[...], sh_ref[...], sc_ref[...]).astype(h_ref.dtype)


def _router_kernel(x_ref, g_ref, sh_ref, sc_ref, wr_ref, br_ref, h_ref, idx_ref, p_ref):
    h = _norm_modulate(x_ref[...], g_ref[...], sh_ref[...], sc_ref[...])
    h_ref[...] = h.astype(h_ref.dtype)
    logits = jnp.dot(h, wr_ref[...], precision=HIGHEST, preferred_element_type=f32) + br_ref[...]
    ne = logits.shape[-1]
    eid = lax.broadcasted_iota(jnp.int32, logits.shape, 1)
    m1 = jnp.max(logits, axis=-1, keepdims=True)
    i1 = jnp.min(jnp.where(logits == m1, eid, ne), axis=-1, keepdims=True)
    rest = jnp.where(eid == i1, -jnp.inf, logits)
    m2 = jnp.max(rest, axis=-1, keepdims=True)
    i2 = jnp.min(jnp.where(rest == m2, eid, ne), axis=-1, keepdims=True)
    e2 = jnp.exp(m2 - m1)
    p1 = 1.0 / (1.0 + e2)
    first = lax.broadcasted_iota(jnp.int32, idx_ref.shape, 1) == 0
    idx_ref[...] = jnp.where(first, i1, i2)
    p_ref[...] = jnp.where(first, p1, e2 * p1)


def _ffn_prep(x, g, mod4, layer, router=None, tm=512):
    b, t, d = x.shape
    common = [
        pl.BlockSpec((None, tm, d), lambda bi, i: (bi, i, 0)),
        pl.BlockSpec((1, d), lambda bi, i: (0, 0)),
        pl.BlockSpec((None, None, 1, d), lambda bi, i: (layer, bi, 0, 3)),
        pl.BlockSpec((None, None, 1, d), lambda bi, i: (layer, bi, 0, 4)),
    ]
    hspec = pl.BlockSpec((None, tm, d), lambda bi, i: (bi, i, 0))
    hshape = jax.ShapeDtypeStruct((b, t, d), bf16)
    if router is None:
        return pl.pallas_call(
            _ffn_prep_kernel, grid=(b, t // tm), in_specs=common, out_specs=hspec, out_shape=hshape,
            compiler_params=_cparams("parallel", "parallel"), name="ffn_prep",
        )(x, g.reshape(1, d), mod4, mod4)
    w_r, b_r = router
    ne = w_r.shape[1]
    return pl.pallas_call(
        _router_kernel,
        grid=(b, t // tm),
        in_specs=common + [
            pl.BlockSpec((d, ne), lambda bi, i: (0, 0)),
            pl.BlockSpec((1, ne), lambda bi, i: (0, 0)),
        ],
        out_specs=[
            hspec,
            pl.BlockSpec((None, tm, TOP_K), lambda bi, i: (bi, i, 0)),
            pl.BlockSpec((None, tm, TOP_K), lambda bi, i: (bi, i, 0)),
        ],
        out_shape=[
            hshape,
            jax.ShapeDtypeStruct((b, t, TOP_K), jnp.int32),
            jax.ShapeDtypeStruct((b, t, TOP_K), f32),
        ],
        compiler_params=_cparams("parallel", "parallel"),
        name="moe_router",
    )(x, g.reshape(1, d), mod4, mod4, w_r, b_r.reshape(1, ne))


def _swiglu_kernel(te_ref, nv_ref, h_ref, rs_ref, wg_ref, wu_ref, wd_ref, o_ref, acc_ref):
    i = pl.program_id(0)
    f = pl.program_id(1)
    nf = pl.num_programs(1)
    valid = i < nv_ref[0]

    @pl.when(f == 0)
    def _():
        acc_ref[...] = jnp.zeros_like(acc_ref)

    @pl.when(valid)
    def _():
        h = h_ref[...]
        a = _silu(jnp.dot(h, wg_ref[...], preferred_element_type=f32)) * jnp.dot(h, wu_ref[...], preferred_element_type=f32)
        acc_ref[...] += jnp.dot(a.astype(bf16), wd_ref[...], preferred_element_type=f32)

    @pl.when(f == nf - 1)
    def _():
        o_ref[...] = (rs_ref[...] * acc_ref[...]).astype(o_ref.dtype)


def _grouped_swiglu(h, rowscale, tile_expert, n_valid, wg, wu, wd, tm=512, tf=512):
    r, d = h.shape
    ff = wg.shape[-1]
    nt = r // tm
    nf = ff // tf

    def wmap(i, f, te, nv):
        return (te[i], 0, jnp.where(i < nv[0], f, nf - 1))

    def wdmap(i, f, te, nv):
        return (te[i], jnp.where(i < nv[0], f, nf - 1), 0)

    grid_spec = pltpu.PrefetchScalarGridSpec(
        num_scalar_prefetch=2,
        grid=(nt, nf),
        in_specs=[
            pl.BlockSpec((tm, d), lambda i, f, te, nv: (i, 0)),
            pl.BlockSpec((tm, 1), lambda i, f, te, nv: (i, 0)),
            pl.BlockSpec((None, d, tf), wmap),
            pl.BlockSpec((None, d, tf), wmap),
            pl.BlockSpec((None, tf, d), wdmap),
        ],
        out_specs=pl.BlockSpec((tm, d), lambda i, f, te, nv: (i, 0)),
        scratch_shapes=[pltpu.VMEM((tm, d), f32)],
    )
    return pl.pallas_call(
        _swiglu_kernel,
        grid_spec=grid_spec,
        out_shape=jax.ShapeDtypeStruct((r, d), bf16),
        compiler_params=_cparams("parallel", "arbitrary"),
        name="grouped_swiglu",
    )(tile_expert, n_valid, h, rowscale, wg, wu, wd)


def _residual_kernel(x_ref, gt_ref, *refs):
    o_ref = refs[-1]
    y = refs[0][...].astype(f32)
    for r in refs[1:-1]:
        y = y + r[...].astype(f32)
    o_ref[...] = x_ref[...] + gt_ref[...] * y


def _gated_residual(x, mod4, layer, ys, tm=512):
    b, t, d = x.shape
    spec = pl.BlockSpec((None, tm, d), lambda bi, i: (bi, i, 0))
    return pl.pallas_call(
        _residual_kernel,
        grid=(b, t // tm),
        in_specs=[spec, pl.BlockSpec((None, None, 1, d), lambda bi, i: (layer, bi, 0, 5))] + [spec] * len(ys),
        out_specs=spec,
        out_shape=jax.ShapeDtypeStruct((b, t, d), f32),
        compiler_params=_cparams("parallel", "parallel"),
        name="gated_residual",
    )(x, mod4, *ys)


def _final_norm_kernel(x_ref, g_ref, o_ref):
    x = x_ref[...]
    o_ref[...] = x * lax.rsqrt(jnp.mean(x * x, axis=-1, keepdims=True) + EPS) * g_ref[...]


def _final_norm(x, g, tm=512):
    b, t, d = x.shape
    spec = pl.BlockSpec((None, tm, d), lambda bi, i: (bi, i, 0))
    return pl.pallas_call(
        _final_norm_kernel,
        grid=(b, t // tm),
        in_specs=[spec, pl.BlockSpec((1, d), lambda bi, i: (0, 0))],
        out_specs=spec,
        out_shape=jax.ShapeDtypeStruct((b, t, d), f32),
        compiler_params=_cparams("parallel", "parallel"),
        name="final_norm",
    )(x, g.reshape(1, d))


def _routing_tables(idx, probs, n_experts, tm):
    n = idx.shape[0]
    na = n * TOP_K
    nt = na // tm + n_experts
    e_flat = idx.reshape(na)
    onehot = (e_flat[:, None] == jnp.arange(n_experts, dtype=jnp.int32)[None, :]).astype(jnp.int32)
    counts = jnp.sum(onehot, axis=0)
    rank = jnp.sum((jnp.cumsum(onehot, axis=0) - 1) * onehot, axis=1)
    tiles_per = (counts + tm - 1) // tm
    tile_end = jnp.cumsum(tiles_per)
    tile_start = tile_end - tiles_per
    slot = tile_start[e_flat] * tm + rank
    n_valid = tile_end[-1:].astype(jnp.int32)
    tile_ids = jnp.arange(nt, dtype=jnp.int32)
    tile_expert = jnp.sum((tile_ids[:, None] >= tile_end[None, :]).astype(jnp.int32), axis=1)
    last_expert = jnp.sum((n_valid - 1 >= tile_end).astype(jnp.int32))
    tile_expert = jnp.where(tile_ids < n_valid, tile_expert, last_expert).astype(jnp.int32)
    token = jnp.arange(na, dtype=jnp.int32) // TOP_K
    row_token = jnp.zeros((nt * tm,), jnp.int32).at[slot].set(token)
    row_scale = jnp.zeros((nt * tm,), f32).at[slot].set(probs.reshape(na))
    return slot.reshape(n, TOP_K), row_token, row_scale, tile_expert, n_valid


def kernel(x, c, w_mod, b_mod, g_mix, g_ffn, w_in, conv_w, conv_b, conv_ln_g, conv_ln_b, ml_conv_w, ml_conv_b, ml_b_i, ml_b_f, ml_norm_g, attn_norm_g, w_out, ffn_w_gate, ffn_w_up, ffn_w_down, moe_w_router, moe_b_router, moe_w_gate, moe_w_up, moe_w_down, g_final):
    b, t, d = x.shape
    depth = w_mod.shape[0]
    conv_ch = conv_w.shape[-1]
    ml_dim = ml_norm_g.shape[-1]
    ml_heads = ml_b_i.shape[-1]
    n_experts = moe_w_router.shape[-1]
    tm_ffn = 512

    mod = _adaln_mod(c, w_mod, b_mod)
    mod4 = mod.reshape(depth, b, 1, 6 * d)

    gate0 = 2 * conv_ch + 4 * ml_dim
    gate1 = gate0 + 2 * ml_heads

    for l in range(depth):
        w_l = w_in[l]
        w_main = jnp.concatenate([w_l[:, :gate0], w_l[:, gate1:]], axis=1).astype(bf16)
        w_gate = w_l[:, gate0:gate1]
        z, zg = _in_projection(x, g_mix[l], mod4, l, w_main, w_gate)
        zg_t = jnp.swapaxes(zg, 1, 2)

        y_conv = _conv_module(z, conv_w[l], conv_b[l], conv_ln_g[l], conv_ln_b[l])
        y_ml = _mlstm(z, zg, zg_t, (2 * conv_ch) // ml_dim, ml_conv_w[l], ml_conv_b[l], ml_b_i[l], ml_b_f[l], ml_norm_g[l])
        y_att = _moba(z, gate0 // HEAD_DIM, attn_norm_g[l])
        x = _out_projection(y_conv, y_ml, y_att, w_out[l].astype(bf16), x, mod4, l)

        if l % 2 == 0:
            j = l // 2
            h = _ffn_prep(x, g_ffn[l], mod4, l, tm=tm_ffn).reshape(b * t, d)
            nt = (b * t) // tm_ffn
            y = _grouped_swiglu(
                h, jnp.ones((b * t, 1), f32), jnp.zeros((nt,), jnp.int32), jnp.full((1,), nt, jnp.int32),
                ffn_w_gate[j].astype(bf16)[None], ffn_w_up[j].astype(bf16)[None], ffn_w_down[j].astype(bf16)[None],
                tm=tm_ffn)
            x = _gated_residual(x, mod4, l, [y.reshape(b, t, d)])
        else:
            j = l // 2
            h, idx, probs = _ffn_prep(x, g_ffn[l], mod4, l, router=(moe_w_router[j], moe_b_router[j]), tm=tm_ffn)
            slot, row_token, row_scale, tile_expert, n_valid = _routing_tables(
                idx.reshape(b * t, TOP_K), probs.reshape(b * t, TOP_K), n_experts, tm_ffn)
            hs = jnp.take(h.reshape(b * t, d), row_token, axis=0)
            ys = _grouped_swiglu(
                hs, row_scale[:, None], tile_expert, n_valid,
                moe_w_gate[j].astype(bf16), moe_w_up[j].astype(bf16), moe_w_down[j].astype(bf16), tm=tm_ffn)
            y0 = jnp.take(ys, slot[:, 0], axis=0).reshape(b, t, d)
            y1 = jnp.take(ys, slot[:, 1], axis=0).reshape(b, t, d)
            x = _gated_residual(x, mod4, l, [y0, y1])

    return _final_norm(x, g_final)
```

```python
import functools

import jax
import jax.numpy as jnp
from jax import lax
from jax.experimental import pallas as pl
from jax.experimental.pallas import tpu as pltpu

f32 = jnp.float32
bf16 = jnp.bfloat16
HIGHEST = lax.Precision.HIGHEST

CONV_GROUPS = 4
HEAD_DIM = 128
ML_CHUNK = 128
MOBA_BLOCK = 256
MOBA_TOPK = 3
TOP_K = 2
EPS = 1e-6
NEG = -1e30

VMEM_LIMIT_BYTES = 56 * 1024 * 1024
LANE = 128


def _cparams(*sem):
    return pltpu.CompilerParams(dimension_semantics=sem, vmem_limit_bytes=VMEM_LIMIT_BYTES)


def _sigmoid(v):
    return 1.0 / (1.0 + jnp.exp(-v))


def _silu(v):
    return v * _sigmoid(v)


def _dot_nt(a, b, **kw):
    return lax.dot_general(a, b, (((1,), (1,)), ((), ())), preferred_element_type=f32, **kw)


def _dot_tn(a, b, **kw):
    return lax.dot_general(a, b, (((0,), (0,)), ((), ())), preferred_element_type=f32, **kw)


def _mod_kernel(c_ref, w_ref, b_ref, o_ref):
    cond = _silu(c_ref[...]).astype(bf16)
    o_ref[...] = jnp.dot(cond, w_ref[...].astype(bf16), preferred_element_type=f32) + b_ref[...]


def _adaln_mod(c, w_mod, b_mod, tn=1024):
    depth, d, n6 = w_mod.shape
    b = c.shape[0]
    return pl.pallas_call(
        _mod_kernel,
        grid=(depth, n6 // tn),
        in_specs=[
            pl.BlockSpec((b, d), lambda l, n: (0, 0)),
            pl.BlockSpec((None, d, tn), lambda l, n: (l, 0, n)),
            pl.BlockSpec((None, 1, tn), lambda l, n: (l, 0, n)),
        ],
        out_specs=pl.BlockSpec((None, b, tn), lambda l, n: (l, 0, n)),
        out_shape=jax.ShapeDtypeStruct((depth, b, n6), f32),
        compiler_params=_cparams("parallel", "parallel"),
        name="adaln_mod",
    )(c, w_mod, b_mod.reshape(depth, 1, n6))


def _norm_modulate(x, g, sh, sc):
    ms = jnp.mean(x * x, axis=-1, keepdims=True)
    h = x * lax.rsqrt(ms + EPS) * g
    return h * (1.0 + sc) + sh


def _prep_kernel(x_ref, g_ref, sh_ref, sc_ref, h_ref):
    h_ref[...] = _norm_modulate(x_ref[...], g_ref[...], sh_ref[...], sc_ref[...]).astype(h_ref.dtype)


def _prep(x, g, mod4, layer, chunk, tm=512):
    b, t, d = x.shape
    return pl.pallas_call(
        _prep_kernel,
        grid=(b, t // tm),
        in_specs=[
            pl.BlockSpec((None, tm, d), lambda bi, i: (bi, i, 0)),
            pl.BlockSpec((1, d), lambda bi, i: (0, 0)),
            pl.BlockSpec((None, None, 1, d), lambda bi, i: (layer, bi, 0, chunk)),
            pl.BlockSpec((None, None, 1, d), lambda bi, i: (layer, bi, 0, chunk + 1)),
        ],
        out_specs=pl.BlockSpec((None, tm, d), lambda bi, i: (bi, i, 0)),
        out_shape=jax.ShapeDtypeStruct((b, t, d), bf16),
        compiler_params=_cparams("parallel", "parallel"),
        name="prep",
    )(x, g.reshape(1, d), mod4, mod4)


def _inproj_kernel(h_ref, wa_ref, wb_ref, wg_ref, z_ref, zg_ref, *, n_a):
    j = pl.program_id(1)

    @pl.when(j == 0)
    def _():
        zg_ref[...] = jnp.dot(h_ref[...], wg_ref[...].astype(bf16), preferred_element_type=f32)

    @pl.when(j < n_a)
    def _():
        z_ref[...] = jnp.dot(h_ref[...], wa_ref[...].astype(bf16), preferred_element_type=f32).astype(bf16)

    @pl.when(j >= n_a)
    def _():
        z_ref[...] = jnp.dot(h_ref[...], wb_ref[...].astype(bf16), preferred_element_type=f32).astype(bf16)


def _in_projection(h, w_in, layer, n_a_cols, w_b, w_gate, tm=1024, tn=512):
    n, d = h.shape
    n_a = n_a_cols // tn
    n_b = w_b.shape[1] // tn
    ng = w_gate.shape[1]
    assert n_a * tn == n_a_cols and n_b * tn == w_b.shape[1]
    return pl.pallas_call(
        functools.partial(_inproj_kernel, n_a=n_a),
        grid=(n // tm, n_a + n_b),
        in_specs=[
            pl.BlockSpec((tm, d), lambda i, j: (i, 0)),
            pl.BlockSpec((None, d, tn), lambda i, j: (layer, 0, jnp.minimum(j, n_a - 1))),
            pl.BlockSpec((d, tn), lambda i, j: (0, jnp.maximum(j - n_a, 0))),
            pl.BlockSpec((d, ng), lambda i, j: (0, 0)),
        ],
        out_specs=[
            pl.BlockSpec((tm, tn), lambda i, j: (i, j)),
            pl.BlockSpec((tm, ng), lambda i, j: (i, 0)),
        ],
        out_shape=[
            jax.ShapeDtypeStruct((n, (n_a + n_b) * tn), bf16),
            jax.ShapeDtypeStruct((n, ng), f32),
        ],
        compiler_params=_cparams("parallel", "arbitrary"),
        name="in_projection",
    )(h, w_in, w_b, w_gate)


CONV_PAD = 32
ROWS = 128


def _conv_kernel(ag_ref, w_ref, b_ref, lg_ref, lb_ref, o_ref, xp_ref):
    t, c = o_ref.shape
    taps = w_ref.shape[0]
    gc = c // CONV_GROUPS
    for cg in range(CONV_GROUPS):
        cs = slice(cg * gc, (cg + 1) * gc)
        gs = slice(c + cg * gc, c + (cg + 1) * gc)
        xp_ref[cg, 0:CONV_PAD, :] = jnp.zeros((CONV_PAD, gc), f32)

        def glu(i, _, cg=cg, cs=cs, gs=gs):
            r0 = pl.multiple_of(i * ROWS, ROWS)
            a = ag_ref[pl.ds(r0, ROWS), cs].astype(f32)
            g = ag_ref[pl.ds(r0, ROWS), gs].astype(f32)
            xp_ref[cg, pl.ds(CONV_PAD + r0, ROWS), :] = a * _sigmoid(g)
            return 0

        lax.fori_loop(0, t // ROWS, glu, 0)

        def body(i, _, cg=cg, cs=cs):
            r0 = pl.multiple_of(i * ROWS, ROWS)
            acc = jnp.zeros((ROWS, gc), f32) + b_ref[:, cs]
            for k in range(taps):
                acc = acc + xp_ref[cg, pl.ds(r0 + (CONV_PAD - (taps - 1) + k), ROWS), :] * w_ref[k:k + 1, cs]
            mu = jnp.mean(acc, axis=-1, keepdims=True)
            dv = acc - mu
            var = jnp.mean(dv * dv, axis=-1, keepdims=True)
            y = dv * lax.rsqrt(var + EPS) * lg_ref[:, cs] + lb_ref[:, cs]
            o_ref[pl.ds(r0, ROWS), cs] = _silu(y).astype(o_ref.dtype)
            return 0

        lax.fori_loop(0, t // ROWS, body, 0)


def _conv_module(z, conv_w, conv_b, ln_g, ln_b):
    b, t, _ = z.shape
    taps, c = conv_w.shape
    assert taps - 1 <= CONV_PAD and c // CONV_GROUPS == LANE
    return pl.pallas_call(
        _conv_kernel,
        grid=(b,),
        in_specs=[
            pl.BlockSpec((None, t, 2 * c), lambda bi: (bi, 0, 0)),
            pl.BlockSpec((taps, c), lambda bi: (0, 0)),
            pl.BlockSpec((1, c), lambda bi: (0, 0)),
            pl.BlockSpec((1, c), lambda bi: (0, 0)),
            pl.BlockSpec((1, c), lambda bi: (0, 0)),
        ],
        out_specs=pl.BlockSpec((None, t, c), lambda bi: (bi, 0, 0)),
        out_shape=jax.ShapeDtypeStruct((b, t, c), bf16),
        scratch_shapes=[pltpu.VMEM((CONV_GROUPS, t + CONV_PAD, c // CONV_GROUPS), f32)],
        compiler_params=_cparams("parallel"),
        name="conv_module",
    )(z, conv_w, conv_b.reshape(1, c), ln_g.reshape(1, c), ln_b.reshape(1, c))


ML_PAD = 8


def _mlstm_kernel(q_ref, k_ref, v_ref, o_ref, gc_ref, gr_ref, cw_ref, cb_ref, bi_ref, bf_ref, bfc_ref,
                  ng_ref, y_ref, qp_ref, kp_ref, qs_ref, ks_ref, ct_ref, n_ref, m_ref):
    t, dm = q_ref.shape
    nh = dm // HEAD_DIM
    L = ML_CHUNK
    nc = t // L
    taps = cw_ref.shape[0]
    kscale = HEAD_DIM ** -0.5

    for h in range(nh):
        hs = slice(h * HEAD_DIM, (h + 1) * HEAD_DIM)
        ks_cols = slice(dm + h * HEAD_DIM, dm + (h + 1) * HEAD_DIM)
        qp_ref[h, 0:ML_PAD, :] = jnp.zeros((ML_PAD, HEAD_DIM), f32)
        kp_ref[h, 0:ML_PAD, :] = jnp.zeros((ML_PAD, HEAD_DIM), f32)

        def stage(i, _, h=h, hs=hs):
            r0 = pl.multiple_of(i * ROWS, ROWS)
            qp_ref[h, pl.ds(ML_PAD + r0, ROWS), :] = q_ref[pl.ds(r0, ROWS), hs].astype(f32)
            kp_ref[h, pl.ds(ML_PAD + r0, ROWS), :] = k_ref[pl.ds(r0, ROWS), hs].astype(f32)
            return 0

        lax.fori_loop(0, t // ROWS, stage, 0)

        def sconv(i, _, h=h, hs=hs, ks_cols=ks_cols):
            r0 = pl.multiple_of(i * ROWS, ROWS)
            aq = jnp.zeros((ROWS, HEAD_DIM), f32) + cb_ref[:, hs]
            ak = jnp.zeros((ROWS, HEAD_DIM), f32) + cb_ref[:, ks_cols]
            for j in range(taps):
                off = ML_PAD - (taps - 1) + j
                aq = aq + qp_ref[h, pl.ds(r0 + off, ROWS), :] * cw_ref[j:j + 1, hs]
                ak = ak + kp_ref[h, pl.ds(r0 + off, ROWS), :] * cw_ref[j:j + 1, ks_cols]
            qs_ref[pl.ds(r0, ROWS), hs] = _silu(aq).astype(bf16)
            ks_ref[pl.ds(r0, ROWS), hs] = (_silu(ak) * kscale).astype(bf16)
            return 0

        lax.fori_loop(0, t // ROWS, sconv, 0)

    ri = lax.broadcasted_iota(jnp.int32, (L, L), 0)
    ci = lax.broadcasted_iota(jnp.int32, (L, L), 1)
    causal = ri >= ci
    lower = causal.astype(bf16)
    upper = (ri <= ci).astype(bf16)

    def log_sigmoid(v):
        return jnp.minimum(v, 0.0) - jnp.log(1.0 + jnp.exp(-jnp.abs(v)))

    def split3(a):
        hi = a.astype(bf16)
        r1 = a - hi.astype(f32)
        mid = r1.astype(bf16)
        lo = (r1 - mid.astype(f32)).astype(bf16)
        return hi, mid, lo

    ct_ref[...] = jnp.zeros_like(ct_ref)
    n_ref[...] = jnp.zeros_like(n_ref)
    m_ref[...] = jnp.zeros_like(m_ref)

    def chunk(i, _):
        r0 = pl.multiple_of(i * L, L)
        lf_col = log_sigmoid(gc_ref[pl.ds(r0, L), nh:2 * nh] + bf_ref[...])
        b_col_all = sum(jnp.dot(lower, p, preferred_element_type=f32) for p in split3(lf_col))
        lf_row = log_sigmoid(gr_ref[nh:2 * nh, pl.ds(r0, L)] + bfc_ref[...])
        b_row_all = sum(jnp.dot(p, upper, preferred_element_type=f32) for p in split3(lf_row))

        for h in range(nh):
            hs = slice(h * HEAD_DIM, (h + 1) * HEAD_DIM)
            ct = ct_ref[h]
            n = n_ref[h]
            m = m_ref[h][:, 0:1]
            q = qs_ref[pl.ds(r0, L), hs]
            k = ks_ref[pl.ds(r0, L), hs]
            v = v_ref[pl.ds(r0, L), hs]
            b_col = b_col_all[:, h:h + 1]
            b_row = b_row_all[h:h + 1, :]
            li_col = gc_ref[pl.ds(r0, L), h:h + 1] + bi_ref[:, h:h + 1]
            li_row = gr_ref[h:h + 1, pl.ds(r0, L)] + bi_ref[:, h:h + 1]

            dmat = jnp.where(causal, b_col - b_row + li_row, NEG)
            inter = b_col + m
            m_t = jnp.maximum(jnp.max(dmat, axis=-1, keepdims=True), inter)
            s = _dot_nt(q, k) * jnp.exp(dmat - m_t)
            w_inter = jnp.exp(inter - m_t)
            qf = q.astype(f32)
            num = jnp.dot(s.astype(bf16), v, preferred_element_type=f32) \
                + w_inter * jnp.dot(q, ct.astype(bf16), preferred_element_type=f32)
            den = jnp.sum(s + (w_inter * qf) * n, axis=-1, keepdims=True)
            hout = num / jnp.maximum(jnp.abs(den), jnp.exp(-m_t))

            b_last = b_row[:, L - 1:L]
            lw_row = b_last - b_row + li_row
            m_new = jnp.maximum(b_last + m, jnp.max(lw_row, axis=-1, keepdims=True))
            wk_col = jnp.exp(b_last - b_col + li_col - m_new)
            decay = jnp.exp(b_last + m - m_new)
            kf = k.astype(f32)
            ct_new = decay * ct + _dot_tn(k, (wk_col * v.astype(f32)).astype(bf16))
            n_new = decay * n + jnp.sum(wk_col * kf, axis=0, keepdims=True)

            hn = hout * lax.rsqrt(jnp.mean(hout * hout, axis=-1, keepdims=True) + EPS) * ng_ref[:, hs]
            og = _sigmoid(o_ref[pl.ds(r0, L), hs].astype(f32))
            y_ref[pl.ds(r0, L), hs] = (og * hn).astype(y_ref.dtype)
            ct_ref[h] = ct_new
            n_ref[h] = n_new
            m_ref[h] = jnp.broadcast_to(m_new, (1, HEAD_DIM))
        return 0

    lax.fori_loop(0, nc, chunk, 0)


def _mlstm(z, zg, zg_t, col0, ml_conv_w, ml_conv_b, b_i, b_f, norm_g):
    b, t, _ = z.shape
    dm = norm_g.shape[0]
    nh = dm // HEAD_DIM
    taps = ml_conv_w.shape[0]
    assert taps - 1 <= ML_PAD

    def zspec(j):
        return pl.BlockSpec((None, t, dm), lambda bi: (bi, 0, col0 + j))

    return pl.pallas_call(
        _mlstm_kernel,
        grid=(b,),
        in_specs=[
            zspec(0), zspec(1), zspec(2), zspec(3),
            pl.BlockSpec((None, t, 2 * nh), lambda bi: (bi, 0, 0)),
            pl.BlockSpec((None, 2 * nh, t), lambda bi: (bi, 0, 0)),
            pl.BlockSpec((taps, 2 * dm), lambda bi: (0, 0)),
            pl.BlockSpec((1, 2 * dm), lambda bi: (0, 0)),
            pl.BlockSpec((1, nh), lambda bi: (0, 0)),
            pl.BlockSpec((1, nh), lambda bi: (0, 0)),
            pl.BlockSpec((nh, 1), lambda bi: (0, 0)),
            pl.BlockSpec((1, dm), lambda bi: (0, 0)),
        ],
        out_specs=pl.BlockSpec((None, t, dm), lambda bi: (bi, 0, 0)),
        out_shape=jax.ShapeDtypeStruct((b, t, dm), bf16),
        scratch_shapes=[
            pltpu.VMEM((nh, t + ML_PAD, HEAD_DIM), f32),
            pltpu.VMEM((nh, t + ML_PAD, HEAD_DIM), f32),
            pltpu.VMEM((t, dm), bf16),
            pltpu.VMEM((t, dm), bf16),
            pltpu.VMEM((nh, HEAD_DIM, HEAD_DIM), f32),
            pltpu.VMEM((nh, 1, HEAD_DIM), f32),
            pltpu.VMEM((nh, 1, HEAD_DIM), f32),
        ],
        compiler_params=_cparams("parallel"),
        name="mlstm",
    )(z, z, z, z, zg, zg_t, ml_conv_w, ml_conv_b.reshape(1, 2 * dm), b_i.reshape(1, nh),
      b_f.reshape(1, nh), b_f.reshape(nh, 1), norm_g.reshape(1, dm))


MASK_BIG = 1e30
BIAS_COLS = 128


def _moba_kernel(q_ref, k_ref, v_ref, kx_ref, sl_ref, ng_ref, y_ref, km_ref, s_ref, kc_ref, vc_ref):
    t, dh = q_ref.shape
    blk = MOBA_BLOCK
    nb = t // blk

    kc_ref[:, 0:dh] = k_ref[...]
    kc_ref[:, dh:dh + BIAS_COLS] = kx_ref[...]
    vc_ref[:, 0:dh] = v_ref[...]
    vc_ref[:, dh:2 * dh] = jnp.ones((t, dh), bf16)

    for n in range(nb):
        km_ref[n:n + 1, :] = jnp.mean(k_ref[n * blk:(n + 1) * blk, :].astype(f32), axis=0, keepdims=True)
    gate_t = _dot_nt(km_ref[...], q_ref[...].astype(f32), precision=HIGHEST)

    blk_id = lax.broadcasted_iota(jnp.int32, (nb, blk), 0)
    ri = lax.broadcasted_iota(jnp.int32, (blk, blk), 0)
    ci = lax.broadcasted_iota(jnp.int32, (blk, blk), 1)
    causal = ri >= ci
    eye = (ri == ci).astype(bf16)
    lane = lax.broadcasted_iota(jnp.int32, (blk, BIAS_COLS), 1)
    rowf = lax.broadcasted_iota(jnp.int32, (blk, BIAS_COLS), 0).astype(f32)
    slope = sl_ref[:, 0:1]

    def logits_pass(j):
        qj = q_ref[j * blk:(j + 1) * blk, :]
        masked_t = jnp.zeros((nb, blk), f32)
        if j > 0:
            g = jnp.where(blk_id < j, gate_t[:, j * blk:(j + 1) * blk], NEG)
            for n in range(j):
                gn = g[n:n + 1, :]
                ahead = (g > gn) | ((g == gn) & (blk_id < n))
                rank = jnp.sum(jnp.where(ahead, 1.0, 0.0), axis=0, keepdims=True)
                masked_t = jnp.where((blk_id == n) & (rank >= float(MOBA_TOPK)), -1.0, masked_t)
        masked_pad = jnp.concatenate([masked_t, jnp.zeros((BIAS_COLS - nb, blk), f32)], axis=0).astype(bf16)
        qx = _dot_nt(eye, masked_pad)
        qx = jnp.where(lane == nb, -slope * float(blk * j), qx)
        qx = jnp.where(lane == nb + 1, -slope * rowf, qx)
        qx = jnp.where((lane == nb + 2) | (lane == nb + 3), 1.0, qx).astype(bf16)
        qc = jnp.concatenate([qj, qx], axis=1)
        mx = None
        for n in range(j + 1):
            ks = slice(n * blk, (n + 1) * blk)
            lg = _dot_nt(qc, kc_ref[ks, :])
            if n == j:
                lg = jnp.where(causal, lg, NEG)
            s_ref[j % 2, :, ks] = lg
            half = jnp.maximum(lg[:, 0:LANE], lg[:, LANE:2 * LANE])
            mx = half if mx is None else jnp.maximum(mx, half)
        return jnp.max(mx, axis=-1, keepdims=True)

    def value_pass(j, m):
        acc = jnp.zeros((blk, 2 * dh), f32)
        for n in range(j + 1):
            ks = slice(n * blk, (n + 1) * blk)
            p = jnp.exp((s_ref[j % 2, :, ks] - m).astype(bf16))
            acc = acc + jnp.dot(p, vc_ref[ks, :], preferred_element_type=f32)
        o = acc[:, 0:dh] / acc[:, dh:2 * dh]
        o = o * lax.rsqrt(jnp.mean(o * o, axis=-1, keepdims=True) + EPS) * ng_ref[...]
        y_ref[j * blk:(j + 1) * blk, :] = o.astype(y_ref.dtype)

    m_prev = logits_pass(0)
    for j in range(1, nb):
        m_cur = logits_pass(j)
        value_pass(j - 1, m_prev)
        m_prev = m_cur
    value_pass(nb - 1, m_prev)


def _moba(z, col0, norm_g):
    b, t, _ = z.shape
    dh = HEAD_DIM
    nh = norm_g.shape[0] // dh
    blk = MOBA_BLOCK
    assert t % blk == 0
    nb = t // blk
    assert nb + 4 <= BIAS_COLS
    slopes = 2.0 ** (-(8.0 / nh) * jnp.arange(1, nh + 1, dtype=f32))
    pos = jnp.arange(t, dtype=jnp.int32)
    col = jnp.arange(BIAS_COLS, dtype=jnp.int32)[None, :]
    kblk = (pos // blk)[:, None]
    base = jnp.where(col == kblk, MASK_BIG, 0.0) + jnp.where((col == nb) | (col == nb + 1), 1.0, 0.0)
    per_head = (jnp.where(col == nb + 2, (kblk * blk).astype(f32), 0.0)
                + jnp.where(col == nb + 3, (pos % blk).astype(f32)[:, None], 0.0))
    kx = (base[None] + slopes[:, None, None] * per_head[None]).astype(bf16)
    slope_rows = jnp.broadcast_to(slopes[:, None, None], (nh, 1, LANE))

    def zspec(j):
        return pl.BlockSpec((None, t, dh), lambda bi, h: (bi, 0, col0 + j * nh + h))

    return pl.pallas_call(
        _moba_kernel,
        grid=(b, nh),
        in_specs=[
            zspec(0), zspec(1), zspec(2),
            pl.BlockSpec((None, t, BIAS_COLS), lambda bi, h: (h, 0, 0)),
            pl.BlockSpec((None, 1, LANE), lambda bi, h: (h, 0, 0)),
            pl.BlockSpec((None, 1, dh), lambda bi, h: (h, 0, 0)),
        ],
        out_specs=pl.BlockSpec((None, t, dh), lambda bi, h: (bi, 0, h)),
        out_shape=jax.ShapeDtypeStruct((b, t, nh * dh), bf16),
        scratch_shapes=[pltpu.VMEM((nb, dh), f32), pltpu.VMEM((2, blk, t), f32), pltpu.VMEM((t, dh + BIAS_COLS), bf16),
                        pltpu.VMEM((t, 2 * dh), bf16)],
        compiler_params=_cparams("parallel", "parallel"),
        name="moba_attention",
    )(z, z, z, kx, slope_rows, norm_g.reshape(nh, 1, dh))


def _router_logits_t(w_t32, h32, hb):
    h_lo = (h32 - hb.astype(f32)).astype(bf16)
    w_hi = w_t32.astype(bf16)
    w_lo = (w_t32 - w_hi.astype(f32)).astype(bf16)
    return _dot_nt(w_hi, hb) + _dot_nt(w_hi, h_lo) + _dot_nt(w_lo, hb)


def _top2_t(logits):
    ne = logits.shape[0]
    eid = lax.broadcasted_iota(jnp.int32, logits.shape, 0)
    m1 = jnp.max(logits, axis=0, keepdims=True)
    i1 = jnp.min(jnp.where(logits == m1, eid, ne), axis=0, keepdims=True)
    rest = jnp.where(eid == i1, -jnp.inf, logits)
    m2 = jnp.max(rest, axis=0, keepdims=True)
    i2 = jnp.min(jnp.where(rest == m2, eid, ne), axis=0, keepdims=True)
    e2 = jnp.exp(m2 - m1)
    p1 = 1.0 / (1.0 + e2)
    return i1, i2, p1, e2 * p1


def _outproj_kernel(yc_ref, ym_ref, ya_ref, w_ref, x_ref, gt_ref, g_ref, sh_ref, sc_ref, *rest, routed):
    if routed:
        wr_ref, br_ref, xo_ref, h_ref, idx_ref, p_ref = rest
    else:
        xo_ref, h_ref = rest
    c1 = yc_ref.shape[-1]
    c2 = ym_ref.shape[-1]
    acc = jnp.dot(yc_ref[...], w_ref[0:c1, :], preferred_element_type=f32)
    acc = acc + jnp.dot(ym_ref[...], w_ref[c1:c1 + c2, :], preferred_element_type=f32)
    acc = acc + jnp.dot(ya_ref[...], w_ref[c1 + c2:, :], preferred_element_type=f32)
    xn = x_ref[...] + gt_ref[...] * acc
    xo_ref[...] = xn
    h = _norm_modulate(xn, g_ref[...], sh_ref[...], sc_ref[...])
    hb = h.astype(bf16)
    h_ref[...] = hb
    if routed:
        i1, i2, p1, p2 = _top2_t(_router_logits_t(wr_ref[...], h, hb) + br_ref[...])
        first = lax.broadcasted_iota(jnp.int32, idx_ref.shape, 0) == 0
        idx_ref[...] = jnp.where(first, i1, i2)
        p_ref[...] = jnp.where(first, p1, p2)


def _out_projection(y_conv, y_ml, y_att, w_out, x, mod4, layer, g_ffn, router=None, tm=512):
    b, t, d = x.shape
    c1, c2, c3 = y_conv.shape[-1], y_ml.shape[-1], y_att.shape[-1]
    assert c1 + c2 + c3 == w_out.shape[0]
    row = lambda c: pl.BlockSpec((None, tm, c), lambda bi, i: (bi, i, 0))
    modspec = lambda chunk: pl.BlockSpec((None, None, 1, d), lambda bi, i: (layer, bi, 0, chunk))
    in_specs = [
        row(c1), row(c2), row(c3),
        pl.BlockSpec(w_out.shape, lambda bi, i: (0, 0)),
        row(d), modspec(2),
        pl.BlockSpec((1, d), lambda bi, i: (0, 0)), modspec(3), modspec(4),
    ]
    args = [y_conv, y_ml, y_att, w_out, x, mod4, g_ffn.reshape(1, d), mod4, mod4]
    out_specs = [row(d), row(d)]
    out_shape = [jax.ShapeDtypeStruct((b, t, d), f32), jax.ShapeDtypeStruct((b, t, d), bf16)]
    if router is not None:
        w_r, b_r = router
        ne = w_r.shape[1]
        in_specs += [pl.BlockSpec((ne, d), lambda bi, i: (0, 0)), pl.BlockSpec((ne, 1), lambda bi, i: (0, 0))]
        args += [w_r.T, b_r.reshape(ne, 1)]
        kspec = pl.BlockSpec((None, TOP_K, tm), lambda bi, i: (bi, 0, i))
        out_specs += [kspec, kspec]
        out_shape += [jax.ShapeDtypeStruct((b, TOP_K, t), jnp.int32), jax.ShapeDtypeStruct((b, TOP_K, t), f32)]
    return pl.pallas_call(
        functools.partial(_outproj_kernel, routed=router is not None),
        grid=(b, t // tm),
        in_specs=in_specs,
        out_specs=out_specs,
        out_shape=out_shape,
        compiler_params=_cparams("parallel", "parallel"),
        name="out_projection",
    )(*args)


def _swiglu_kernel(te_ref, nv_ref, h_ref, rs_ref, wg_ref, wu_ref, wd_ref, o_ref, acc_ref):
    i = pl.program_id(0)
    f = pl.program_id(1)
    nf = pl.num_programs(1)
    valid = i < nv_ref[0]

    @pl.when(f == 0)
    def _():
        acc_ref[...] = jnp.zeros_like(acc_ref)

    @pl.when(valid)
    def _():
        h = h_ref[...]
        g = jnp.dot(h, wg_ref[...].astype(bf16), preferred_element_type=f32)
        u = jnp.dot(h, wu_ref[...].astype(bf16), preferred_element_type=f32)
        a = (_silu(g) * u).astype(bf16)
        acc_ref[...] += jnp.dot(a, wd_ref[...].astype(bf16), preferred_element_type=f32)

    @pl.when(f == nf - 1)
    def _():
        o_ref[...] = (rs_ref[...] * acc_ref[...]).astype(o_ref.dtype)


def _grouped_swiglu(h, rowscale, tile_expert, n_valid, wg, wu, wd, tm=1024, tf=256):
    r, d = h.shape
    ff = wg.shape[-1]
    nt = r // tm
    nf = ff // tf

    def wmap(i, f, te, nv):
        return (te[i], 0, jnp.where(i < nv[0], f, nf - 1))

    def wdmap(i, f, te, nv):
        return (te[i], jnp.where(i < nv[0], f, nf - 1), 0)

    grid_spec = pltpu.PrefetchScalarGridSpec(
        num_scalar_prefetch=2,
        grid=(nt, nf),
        in_specs=[
            pl.BlockSpec((tm, d), lambda i, f, te, nv: (i, 0)),
            pl.BlockSpec((tm, 1), lambda i, f, te, nv: (i, 0)),
            pl.BlockSpec((None, d, tf), wmap),
            pl.BlockSpec((None, d, tf), wmap),
            pl.BlockSpec((None, tf, d), wdmap),
        ],
        out_specs=pl.BlockSpec((tm, d), lambda i, f, te, nv: (i, 0)),
        scratch_shapes=[pltpu.VMEM((tm, d), f32)],
    )
    return pl.pallas_call(
        _swiglu_kernel,
        grid_spec=grid_spec,
        out_shape=jax.ShapeDtypeStruct((r, d), bf16),
        compiler_params=_cparams("parallel", "arbitrary"),
        name="grouped_swiglu",
    )(tile_expert, n_valid, h, rowscale, wg, wu, wd)


def _combine_kernel(*refs, n_y, last):
    x_ref, gt_ref = refs[0], refs[1]
    y = refs[2][...].astype(f32)
    for r in refs[3:2 + n_y]:
        y = y + r[...].astype(f32)
    xn = x_ref[...] + gt_ref[...] * y
    rest = refs[2 + n_y:]
    if last:
        g_ref, o_ref = rest
        o_ref[...] = xn * lax.rsqrt(jnp.mean(xn * xn, axis=-1, keepdims=True) + EPS) * g_ref[...]
    else:
        g_ref, sh_ref, sc_ref, xo_ref, h_ref = rest
        xo_ref[...] = xn
        h_ref[...] = _norm_modulate(xn, g_ref[...], sh_ref[...], sc_ref[...]).astype(h_ref.dtype)


def _combine(x, mod4, layer, ys, g_next, last, tm=512):
    b, t, d = x.shape
    spec = pl.BlockSpec((None, tm, d), lambda bi, i: (bi, i, 0))
    modspec = lambda lyr, chunk: pl.BlockSpec((None, None, 1, d), lambda bi, i: (lyr, bi, 0, chunk))
    in_specs = [spec, modspec(layer, 5)] + [spec] * len(ys) + [pl.BlockSpec((1, d), lambda bi, i: (0, 0))]
    args = [x, mod4, *ys, g_next.reshape(1, d)]
    if last:
        out_specs, out_shape = spec, jax.ShapeDtypeStruct((b, t, d), f32)
    else:
        in_specs += [modspec(layer + 1, 0), modspec(layer + 1, 1)]
        args += [mod4, mod4]
        out_specs = [spec, spec]
        out_shape = [jax.ShapeDtypeStruct((b, t, d), f32), jax.ShapeDtypeStruct((b, t, d), bf16)]
    return pl.pallas_call(
        functools.partial(_combine_kernel, n_y=len(ys), last=last),
        grid=(b, t // tm),
        in_specs=in_specs,
        out_specs=out_specs,
        out_shape=out_shape,
        compiler_params=_cparams("parallel", "parallel"),
        name="ffn_combine",
    )(*args)


def _routing_tables(idx, probs, n_experts, tm):
    n = idx.shape[0]
    na = n * TOP_K
    nt = na // tm + n_experts
    e_flat = idx.reshape(na)
    onehot = (e_flat[:, None] == jnp.arange(n_experts, dtype=jnp.int32)[None, :]).astype(jnp.int32)
    counts = jnp.sum(onehot, axis=0)
    rank = jnp.sum((jnp.cumsum(onehot, axis=0) - 1) * onehot, axis=1)
    tiles_per = (counts + tm - 1) // tm
    tile_end = jnp.cumsum(tiles_per)
    tile_start = tile_end - tiles_per
    slot = tile_start[e_flat] * tm + rank
    n_valid = tile_end[-1:].astype(jnp.int32)
    tile_ids = jnp.arange(nt, dtype=jnp.int32)
    tile_expert = jnp.sum((tile_ids[:, None] >= tile_end[None, :]).astype(jnp.int32), axis=1)
    last_expert = jnp.sum((n_valid - 1 >= tile_end).astype(jnp.int32))
    tile_expert = jnp.where(tile_ids < n_valid, tile_expert, last_expert).astype(jnp.int32)
    token = jnp.arange(na, dtype=jnp.int32) // TOP_K
    row_token = jnp.zeros((nt * tm,), jnp.int32).at[slot].set(token)
    row_scale = jnp.zeros((nt * tm,), f32).at[slot].set(probs.reshape(na))
    return slot.reshape(n, TOP_K), row_token, row_scale, tile_expert, n_valid


def kernel(x, c, w_mod, b_mod, g_mix, g_ffn, w_in, conv_w, conv_b, conv_ln_g, conv_ln_b, ml_conv_w, ml_conv_b, ml_b_i, ml_b_f, ml_norm_g, attn_norm_g, w_out, ffn_w_gate, ffn_w_up, ffn_w_down, moe_w_router, moe_b_router, moe_w_gate, moe_w_up, moe_w_down, g_final):
    b, t, d = x.shape
    depth = w_mod.shape[0]
    conv_ch = conv_w.shape[-1]
    ml_dim = ml_norm_g.shape[-1]
    ml_heads = ml_b_i.shape[-1]
    n_experts = moe_w_router.shape[-1]
    att_dim = attn_norm_g.shape[-1]
    n_tok = b * t
    tm_ffn = 1024

    mod = _adaln_mod(c, w_mod, b_mod)
    mod4 = mod.reshape(depth, b, 1, 6 * d)

    gate0 = 2 * conv_ch + 4 * ml_dim
    gate1 = gate0 + 2 * ml_heads
    att_col_scale = jnp.concatenate([jnp.full((att_dim,), HEAD_DIM ** -0.5, f32), jnp.ones((2 * att_dim,), f32)])

    h = _prep(x, g_mix[0], mod4, 0, 0)
    out = None
    for l in range(depth):
        w_att = w_in[l, :, gate1:] * att_col_scale[None, :]
        w_gate = w_in[l, :, gate0:gate1]
        z, zg = _in_projection(h.reshape(n_tok, d), w_in, l, gate0, w_att, w_gate)
        z = z.reshape(b, t, -1)
        zg = zg.reshape(b, t, -1)
        zg_t = jnp.swapaxes(zg, 1, 2)

        y_conv = _conv_module(z, conv_w[l], conv_b[l], conv_ln_g[l], conv_ln_b[l])
        y_ml = _mlstm(z, zg, zg_t, (2 * conv_ch) // ml_dim, ml_conv_w[l], ml_conv_b[l], ml_b_i[l], ml_b_f[l], ml_norm_g[l])
        y_att = _moba(z, gate0 // HEAD_DIM, attn_norm_g[l])

        j = l // 2
        w_o = w_out[l].astype(bf16)
        if l % 2 == 0:
            x, h = _out_projection(y_conv, y_ml, y_att, w_o, x, mod4, l, g_ffn[l])
            nt = n_tok // tm_ffn
            y = _grouped_swiglu(
                h.reshape(n_tok, d), jnp.ones((n_tok, 1), f32), jnp.full((nt,), j, jnp.int32), jnp.full((1,), nt, jnp.int32),
                ffn_w_gate, ffn_w_up, ffn_w_down, tm=tm_ffn)
            ys = [y.reshape(b, t, d)]
        else:
            x, h, idx, probs = _out_projection(y_conv, y_ml, y_att, w_o, x, mod4, l, g_ffn[l],
                                               router=(moe_w_router[j], moe_b_router[j]))
            slot, row_token, row_scale, tile_expert, n_valid = _routing_tables(
                jnp.swapaxes(idx, 1, 2).reshape(n_tok, TOP_K), jnp.swapaxes(probs, 1, 2).reshape(n_tok, TOP_K),
                n_experts, tm_ffn)
            hs = jnp.take(h.reshape(n_tok, d), row_token, axis=0)
            ff = moe_w_gate.shape[-1]
            y_sorted = _grouped_swiglu(
                hs, row_scale[:, None], tile_expert + j * n_experts, n_valid,
                moe_w_gate.reshape(-1, d, ff), moe_w_up.reshape(-1, d, ff), moe_w_down.reshape(-1, ff, d), tm=tm_ffn)
            ys = [jnp.take(y_sorted, slot[:, k], axis=0).reshape(b, t, d) for k in range(TOP_K)]

        if l + 1 < depth:
            x, h = _combine(x, mod4, l, ys, g_mix[l + 1], last=False)
        else:
            out = _combine(x, mod4, l, ys, g_final, last=True)
    return out
```

```python
import functools

import jax
import jax.numpy as jnp
from jax import lax
from jax.experimental import pallas as pl
from jax.experimental.pallas import tpu as pltpu

f32 = jnp.float32
bf16 = jnp.bfloat16
HIGHEST = lax.Precision.HIGHEST

CONV_GROUPS = 4
HEAD_DIM = 128
ML_CHUNK = 128
MOBA_BLOCK = 256
MOBA_TOPK = 3
TOP_K = 2
EPS = 1e-6
NEG = -1e30

VMEM_LIMIT_BYTES = 56 * 1024 * 1024
LANE = 128


def _cparams(*sem):
    return pltpu.CompilerParams(dimension_semantics=sem, vmem_limit_bytes=VMEM_LIMIT_BYTES)


def _sigmoid(v):
    return 1.0 / (1.0 + jnp.exp(-v))


def _silu(v):
    return v * _sigmoid(v)


def _dot_nt(a, b, **kw):
    return lax.dot_general(a, b, (((1,), (1,)), ((), ())), preferred_element_type=f32, **kw)


def _dot_tn(a, b, **kw):
    return lax.dot_general(a, b, (((0,), (0,)), ((), ())), preferred_element_type=f32, **kw)


def _mod_kernel(c_ref, w_ref, b_ref, o_ref):
    cond = _silu(c_ref[...]).astype(bf16)
    o_ref[...] = jnp.dot(cond, w_ref[...].astype(bf16), preferred_element_type=f32) + b_ref[...]


def _adaln_mod(c, w_mod, b_mod, tn=1024):
    depth, d, n6 = w_mod.shape
    b = c.shape[0]
    return pl.pallas_call(
        _mod_kernel,
        grid=(depth, n6 // tn),
        in_specs=[
            pl.BlockSpec((b, d), lambda l, n: (0, 0)),
            pl.BlockSpec((None, d, tn), lambda l, n: (l, 0, n)),
            pl.BlockSpec((None, 1, tn), lambda l, n: (l, 0, n)),
        ],
        out_specs=pl.BlockSpec((None, b, tn), lambda l, n: (l, 0, n)),
        out_shape=jax.ShapeDtypeStruct((depth, b, n6), f32),
        compiler_params=_cparams("parallel", "parallel"),
        name="adaln_mod",
    )(c, w_mod, b_mod.reshape(depth, 1, n6))


def _norm_modulate(x, g, sh, sc):
    ms = jnp.mean(x * x, axis=-1, keepdims=True)
    h = x * lax.rsqrt(ms + EPS) * g
    return h * (1.0 + sc) + sh


def _prep_kernel(x_ref, g_ref, sh_ref, sc_ref, h_ref):
    h_ref[...] = _norm_modulate(x_ref[...], g_ref[...], sh_ref[...], sc_ref[...]).astype(h_ref.dtype)


def _prep(x, g, mod4, layer, chunk, tm=512):
    b, t, d = x.shape
    return pl.pallas_call(
        _prep_kernel,
        grid=(b, t // tm),
        in_specs=[
            pl.BlockSpec((None, tm, d), lambda bi, i: (bi, i, 0)),
            pl.BlockSpec((1, d), lambda bi, i: (0, 0)),
            pl.BlockSpec((None, None, 1, d), lambda bi, i: (layer, bi, 0, chunk)),
            pl.BlockSpec((None, None, 1, d), lambda bi, i: (layer, bi, 0, chunk + 1)),
        ],
        out_specs=pl.BlockSpec((None, tm, d), lambda bi, i: (bi, i, 0)),
        out_shape=jax.ShapeDtypeStruct((b, t, d), bf16),
        compiler_params=_cparams("parallel", "parallel"),
        name="prep",
    )(x, g.reshape(1, d), mod4, mod4)


def _inproj_kernel(h_ref, wa_ref, wb_ref, wg_ref, z_ref, zg_ref, *, n_a):
    j = pl.program_id(1)

    @pl.when(j == 0)
    def _():
        zg_ref[...] = jnp.dot(h_ref[...], wg_ref[...].astype(bf16), preferred_element_type=f32)

    @pl.when(j < n_a)
    def _():
        z_ref[...] = jnp.dot(h_ref[...], wa_ref[...].astype(bf16), preferred_element_type=f32).astype(bf16)

    @pl.when(j >= n_a)
    def _():
        z_ref[...] = jnp.dot(h_ref[...], wb_ref[...].astype(bf16), preferred_element_type=f32).astype(bf16)


def _in_projection(h, w_in, layer, n_a_cols, w_b, w_gate, tm=2048, tn=512):
    n, d = h.shape
    n_a = n_a_cols // tn
    n_b = w_b.shape[1] // tn
    ng = w_gate.shape[1]
    assert n_a * tn == n_a_cols and n_b * tn == w_b.shape[1]
    return pl.pallas_call(
        functools.partial(_inproj_kernel, n_a=n_a),
        grid=(n // tm, n_a + n_b),
        in_specs=[
            pl.BlockSpec((tm, d), lambda i, j: (i, 0)),
            pl.BlockSpec((None, d, tn), lambda i, j: (layer, 0, jnp.minimum(j, n_a - 1))),
            pl.BlockSpec((d, tn), lambda i, j: (0, jnp.maximum(j - n_a, 0))),
            pl.BlockSpec((d, ng), lambda i, j: (0, 0)),
        ],
        out_specs=[
            pl.BlockSpec((tm, tn), lambda i, j: (i, j)),
            pl.BlockSpec((tm, ng), lambda i, j: (i, 0)),
        ],
        out_shape=[
            jax.ShapeDtypeStruct((n, (n_a + n_b) * tn), bf16),
            jax.ShapeDtypeStruct((n, ng), f32),
        ],
        compiler_params=_cparams("parallel", "arbitrary"),
        name="in_projection",
    )(h, w_in, w_b, w_gate)


CONV_PAD = 32
ROWS = 128


def _conv_kernel(ag_ref, w_ref, b_ref, lg_ref, lb_ref, o_ref, xp_ref):
    t, c = o_ref.shape
    taps = w_ref.shape[0]
    gc = c // CONV_GROUPS
    for cg in range(CONV_GROUPS):
        cs = slice(cg * gc, (cg + 1) * gc)
        gs = slice(c + cg * gc, c + (cg + 1) * gc)
        xp_ref[cg, 0:CONV_PAD, :] = jnp.zeros((CONV_PAD, gc), f32)

        def glu(i, _, cg=cg, cs=cs, gs=gs):
            r0 = pl.multiple_of(i * ROWS, ROWS)
            a = ag_ref[pl.ds(r0, ROWS), cs].astype(f32)
            g = ag_ref[pl.ds(r0, ROWS), gs].astype(f32)
            xp_ref[cg, pl.ds(CONV_PAD + r0, ROWS), :] = a * _sigmoid(g)
            return 0

        lax.fori_loop(0, t // ROWS, glu, 0)

        def body(i, _, cg=cg, cs=cs):
            r0 = pl.multiple_of(i * ROWS, ROWS)
            acc = jnp.zeros((ROWS, gc), f32) + b_ref[:, cs]
            for k in range(taps):
                acc = acc + xp_ref[cg, pl.ds(r0 + (CONV_PAD - (taps - 1) + k), ROWS), :] * w_ref[k:k + 1, cs]
            mu = jnp.mean(acc, axis=-1, keepdims=True)
            dv = acc - mu
            var = jnp.mean(dv * dv, axis=-1, keepdims=True)
            y = dv * lax.rsqrt(var + EPS) * lg_ref[:, cs] + lb_ref[:, cs]
            o_ref[pl.ds(r0, ROWS), cs] = _silu(y).astype(o_ref.dtype)
            return 0

        lax.fori_loop(0, t // ROWS, body, 0, unroll=2)


def _conv_module(z, conv_w, conv_b, ln_g, ln_b):
    b, t, _ = z.shape
    taps, c = conv_w.shape
    assert taps - 1 <= CONV_PAD and c // CONV_GROUPS == LANE
    return pl.pallas_call(
        _conv_kernel,
        grid=(b,),
        in_specs=[
            pl.BlockSpec((None, t, 2 * c), lambda bi: (bi, 0, 0)),
            pl.BlockSpec((taps, c), lambda bi: (0, 0)),
            pl.BlockSpec((1, c), lambda bi: (0, 0)),
            pl.BlockSpec((1, c), lambda bi: (0, 0)),
            pl.BlockSpec((1, c), lambda bi: (0, 0)),
        ],
        out_specs=pl.BlockSpec((None, t, c), lambda bi: (bi, 0, 0)),
        out_shape=jax.ShapeDtypeStruct((b, t, c), bf16),
        scratch_shapes=[pltpu.VMEM((CONV_GROUPS, t + CONV_PAD, c // CONV_GROUPS), f32)],
        compiler_params=_cparams("parallel"),
        name="conv_module",
    )(z, conv_w, conv_b.reshape(1, c), ln_g.reshape(1, c), ln_b.reshape(1, c))


ML_PAD = 8


def _mlstm_kernel(q_ref, k_ref, v_ref, o_ref, gc_ref, gr_ref, cw_ref, cb_ref, bi_ref, bf_ref, bfc_ref,
                  ng_ref, y_ref, qp_ref, kp_ref, qs_ref, ks_ref, ct_ref, n_ref, m_ref):
    t, dm = q_ref.shape
    nh = dm // HEAD_DIM
    L = ML_CHUNK
    nc = t // L
    taps = cw_ref.shape[0]
    kscale = HEAD_DIM ** -0.5

    for h in range(nh):
        hs = slice(h * HEAD_DIM, (h + 1) * HEAD_DIM)
        ks_cols = slice(dm + h * HEAD_DIM, dm + (h + 1) * HEAD_DIM)
        qp_ref[h, 0:ML_PAD, :] = jnp.zeros((ML_PAD, HEAD_DIM), f32)
        kp_ref[h, 0:ML_PAD, :] = jnp.zeros((ML_PAD, HEAD_DIM), f32)

        def stage(i, _, h=h, hs=hs):
            r0 = pl.multiple_of(i * ROWS, ROWS)
            qp_ref[h, pl.ds(ML_PAD + r0, ROWS), :] = q_ref[pl.ds(r0, ROWS), hs].astype(f32)
            kp_ref[h, pl.ds(ML_PAD + r0, ROWS), :] = k_ref[pl.ds(r0, ROWS), hs].astype(f32)
            return 0

        lax.fori_loop(0, t // ROWS, stage, 0)

        def sconv(i, _, h=h, hs=hs, ks_cols=ks_cols):
            r0 = pl.multiple_of(i * ROWS, ROWS)
            aq = jnp.zeros((ROWS, HEAD_DIM), f32) + cb_ref[:, hs]
            ak = jnp.zeros((ROWS, HEAD_DIM), f32) + cb_ref[:, ks_cols]
            for j in range(taps):
                off = ML_PAD - (taps - 1) + j
                aq = aq + qp_ref[h, pl.ds(r0 + off, ROWS), :] * cw_ref[j:j + 1, hs]
                ak = ak + kp_ref[h, pl.ds(r0 + off, ROWS), :] * cw_ref[j:j + 1, ks_cols]
            qs_ref[pl.ds(r0, ROWS), hs] = _silu(aq).astype(bf16)
            ks_ref[pl.ds(r0, ROWS), hs] = (_silu(ak) * kscale).astype(bf16)
            return 0

        lax.fori_loop(0, t // ROWS, sconv, 0)

    ri = lax.broadcasted_iota(jnp.int32, (L, L), 0)
    ci = lax.broadcasted_iota(jnp.int32, (L, L), 1)
    causal = ri >= ci
    lower = causal.astype(bf16)
    upper = (ri <= ci).astype(bf16)

    def log_sigmoid(v):
        return jnp.minimum(v, 0.0) - jnp.log(1.0 + jnp.exp(-jnp.abs(v)))

    def split3(a):
        hi = a.astype(bf16)
        r1 = a - hi.astype(f32)
        mid = r1.astype(bf16)
        lo = (r1 - mid.astype(f32)).astype(bf16)
        return hi, mid, lo

    ct_ref[...] = jnp.zeros_like(ct_ref)
    n_ref[...] = jnp.zeros_like(n_ref)
    m_ref[...] = jnp.zeros_like(m_ref)

    def chunk(i, _):
        r0 = pl.multiple_of(i * L, L)
        lf_col = log_sigmoid(gc_ref[pl.ds(r0, L), nh:2 * nh] + bf_ref[...])
        b_col_all = sum(jnp.dot(lower, p, preferred_element_type=f32) for p in split3(lf_col))
        lf_row = log_sigmoid(gr_ref[nh:2 * nh, pl.ds(r0, L)] + bfc_ref[...])
        b_row_all = sum(jnp.dot(p, upper, preferred_element_type=f32) for p in split3(lf_row))

        for h in range(nh):
            hs = slice(h * HEAD_DIM, (h + 1) * HEAD_DIM)
            ct = ct_ref[h]
            n = n_ref[h]
            m = m_ref[h][:, 0:1]
            q = qs_ref[pl.ds(r0, L), hs]
            k = ks_ref[pl.ds(r0, L), hs]
            v = v_ref[pl.ds(r0, L), hs]
            b_col = b_col_all[:, h:h + 1]
            b_row = b_row_all[h:h + 1, :]
            li_col = gc_ref[pl.ds(r0, L), h:h + 1] + bi_ref[:, h:h + 1]
            li_row = gr_ref[h:h + 1, pl.ds(r0, L)] + bi_ref[:, h:h + 1]

            dmat = jnp.where(causal, b_col - b_row + li_row, NEG)
            inter = b_col + m
            m_t = jnp.maximum(jnp.max(dmat, axis=-1, keepdims=True), inter)
            s = _dot_nt(q, k) * jnp.exp(dmat - m_t)
            w_inter = jnp.exp(inter - m_t)
            qf = q.astype(f32)
            num = jnp.dot(s.astype(bf16), v, preferred_element_type=f32) \
                + w_inter * jnp.dot(q, ct.astype(bf16), preferred_element_type=f32)
            den = jnp.sum(s + (w_inter * qf) * n, axis=-1, keepdims=True)
            hout = num / jnp.maximum(jnp.abs(den), jnp.exp(-m_t))

            b_last = b_row[:, L - 1:L]
            lw_row = b_last - b_row + li_row
            m_new = jnp.maximum(b_last + m, jnp.max(lw_row, axis=-1, keepdims=True))
            wk_col = jnp.exp(b_last - b_col + li_col - m_new)
            decay = jnp.exp(b_last + m - m_new)
            kf = k.astype(f32)
            ct_new = decay * ct + _dot_tn(k, (wk_col * v.astype(f32)).astype(bf16))
            n_new = decay * n + jnp.sum(wk_col * kf, axis=0, keepdims=True)

            hn = hout * lax.rsqrt(jnp.mean(hout * hout, axis=-1, keepdims=True) + EPS) * ng_ref[:, hs]
            og = _sigmoid(o_ref[pl.ds(r0, L), hs].astype(f32))
            y_ref[pl.ds(r0, L), hs] = (og * hn).astype(y_ref.dtype)
            ct_ref[h] = ct_new
            n_ref[h] = n_new
            m_ref[h] = jnp.broadcast_to(m_new, (1, HEAD_DIM))
        return 0

    lax.fori_loop(0, nc, chunk, 0)


def _mlstm(z, zg, zg_t, col0, ml_conv_w, ml_conv_b, b_i, b_f, norm_g):
    b, t, _ = z.shape
    dm = norm_g.shape[0]
    nh = dm // HEAD_DIM
    taps = ml_conv_w.shape[0]
    assert taps - 1 <= ML_PAD

    def zspec(j):
        return pl.BlockSpec((None, t, dm), lambda bi: (bi, 0, col0 + j))

    return pl.pallas_call(
        _mlstm_kernel,
        grid=(b,),
        in_specs=[
            zspec(0), zspec(1), zspec(2), zspec(3),
            pl.BlockSpec((None, t, 2 * nh), lambda bi: (bi, 0, 0)),
            pl.BlockSpec((None, 2 * nh, t), lambda bi: (bi, 0, 0)),
            pl.BlockSpec((taps, 2 * dm), lambda bi: (0, 0)),
            pl.BlockSpec((1, 2 * dm), lambda bi: (0, 0)),
            pl.BlockSpec((1, nh), lambda bi: (0, 0)),
            pl.BlockSpec((1, nh), lambda bi: (0, 0)),
            pl.BlockSpec((nh, 1), lambda bi: (0, 0)),
            pl.BlockSpec((1, dm), lambda bi: (0, 0)),
        ],
        out_specs=pl.BlockSpec((None, t, dm), lambda bi: (bi, 0, 0)),
        out_shape=jax.ShapeDtypeStruct((b, t, dm), bf16),
        scratch_shapes=[
            pltpu.VMEM((nh, t + ML_PAD, HEAD_DIM), f32),
            pltpu.VMEM((nh, t + ML_PAD, HEAD_DIM), f32),
            pltpu.VMEM((t, dm), bf16),
            pltpu.VMEM((t, dm), bf16),
            pltpu.VMEM((nh, HEAD_DIM, HEAD_DIM), f32),
            pltpu.VMEM((nh, 1, HEAD_DIM), f32),
            pltpu.VMEM((nh, 1, HEAD_DIM), f32),
        ],
        compiler_params=_cparams("parallel"),
        name="mlstm",
    )(z, z, z, z, zg, zg_t, ml_conv_w, ml_conv_b.reshape(1, 2 * dm), b_i.reshape(1, nh),
      b_f.reshape(1, nh), b_f.reshape(nh, 1), norm_g.reshape(1, dm))


MASK_BIG = 1e30
BIAS_COLS = 128


def _moba_kernel(q_ref, k_ref, v_ref, kx_ref, sl_ref, ng_ref, y_ref, km_ref, s_ref, kc_ref, vc_ref):
    t, dh = q_ref.shape
    blk = MOBA_BLOCK
    nb = t // blk

    kc_ref[:, 0:dh] = k_ref[...]
    kc_ref[:, dh:dh + BIAS_COLS] = kx_ref[...]
    vc_ref[:, 0:dh] = v_ref[...]
    vc_ref[:, dh:2 * dh] = jnp.ones((t, dh), bf16)

    for n in range(nb):
        km_ref[n:n + 1, :] = jnp.mean(k_ref[n * blk:(n + 1) * blk, :].astype(f32), axis=0, keepdims=True)
    gate_t = _dot_nt(km_ref[...], q_ref[...].astype(f32), precision=HIGHEST)

    blk_id = lax.broadcasted_iota(jnp.int32, (nb, blk), 0)
    ri = lax.broadcasted_iota(jnp.int32, (blk, blk), 0)
    ci = lax.broadcasted_iota(jnp.int32, (blk, blk), 1)
    causal = ri >= ci
    eye = (ri == ci).astype(bf16)
    lane = lax.broadcasted_iota(jnp.int32, (blk, BIAS_COLS), 1)
    rowf = lax.broadcasted_iota(jnp.int32, (blk, BIAS_COLS), 0).astype(f32)
    slope = sl_ref[:, 0:1]

    def logits_pass(j):
        qj = q_ref[j * blk:(j + 1) * blk, :]
        masked_t = jnp.zeros((nb, blk), f32)
        if j > 0:
            g = jnp.where(blk_id < j, gate_t[:, j * blk:(j + 1) * blk], NEG)
            for n in range(j):
                gn = g[n:n + 1, :]
                ahead = (g > gn) | ((g == gn) & (blk_id < n))
                rank = jnp.sum(jnp.where(ahead, 1.0, 0.0), axis=0, keepdims=True)
                masked_t = jnp.where((blk_id == n) & (rank >= float(MOBA_TOPK)), -1.0, masked_t)
        masked_pad = jnp.concatenate([masked_t, jnp.zeros((BIAS_COLS - nb, blk), f32)], axis=0).astype(bf16)
        qx = _dot_nt(eye, masked_pad)
        qx = jnp.where(lane == nb, -slope * float(blk * j), qx)
        qx = jnp.where(lane == nb + 1, -slope * rowf, qx)
        qx = jnp.where((lane == nb + 2) | (lane == nb + 3), 1.0, qx).astype(bf16)
        qc = jnp.concatenate([qj, qx], axis=1)
        mx = None
        for n in range(j + 1):
            ks = slice(n * blk, (n + 1) * blk)
            lg = _dot_nt(qc, kc_ref[ks, :])
            if n == j:
                lg = jnp.where(causal, lg, NEG)
            s_ref[j % 2, :, ks] = lg
            half = jnp.maximum(lg[:, 0:LANE], lg[:, LANE:2 * LANE])
            mx = half if mx is None else jnp.maximum(mx, half)
        return jnp.max(mx, axis=-1, keepdims=True)

    def value_pass(j, m):
        acc = jnp.zeros((blk, 2 * dh), f32)
        for n in range(j + 1):
            ks = slice(n * blk, (n + 1) * blk)
            p = jnp.exp((s_ref[j % 2, :, ks] - m).astype(bf16))
            acc = acc + jnp.dot(p, vc_ref[ks, :], preferred_element_type=f32)
        o = acc[:, 0:dh] / acc[:, dh:2 * dh]
        o = o * lax.rsqrt(jnp.mean(o * o, axis=-1, keepdims=True) + EPS) * ng_ref[...]
        y_ref[j * blk:(j + 1) * blk, :] = o.astype(y_ref.dtype)

    m_prev = logits_pass(0)
    for j in range(1, nb):
        m_cur = logits_pass(j)
        value_pass(j - 1, m_prev)
        m_prev = m_cur
    value_pass(nb - 1, m_prev)


def _moba(z, col0, norm_g):
    b, t, _ = z.shape
    dh = HEAD_DIM
    nh = norm_g.shape[0] // dh
    blk = MOBA_BLOCK
    assert t % blk == 0
    nb = t // blk
    assert nb + 4 <= BIAS_COLS
    slopes = 2.0 ** (-(8.0 / nh) * jnp.arange(1, nh + 1, dtype=f32))
    pos = jnp.arange(t, dtype=jnp.int32)
    col = jnp.arange(BIAS_COLS, dtype=jnp.int32)[None, :]
    kblk = (pos // blk)[:, None]
    base = jnp.where(col == kblk, MASK_BIG, 0.0) + jnp.where((col == nb) | (col == nb + 1), 1.0, 0.0)
    per_head = (jnp.where(col == nb + 2, (kblk * blk).astype(f32), 0.0)
                + jnp.where(col == nb + 3, (pos % blk).astype(f32)[:, None], 0.0))
    kx = (base[None] + slopes[:, None, None] * per_head[None]).astype(bf16)
    slope_rows = jnp.broadcast_to(slopes[:, None, None], (nh, 1, LANE))

    def zspec(j):
        return pl.BlockSpec((None, t, dh), lambda bi, h: (bi, 0, col0 + j * nh + h))

    return pl.pallas_call(
        _moba_kernel,
        grid=(b, nh),
        in_specs=[
            zspec(0), zspec(1), zspec(2),
            pl.BlockSpec((None, t, BIAS_COLS), lambda bi, h: (h, 0, 0)),
            pl.BlockSpec((None, 1, LANE), lambda bi, h: (h, 0, 0)),
            pl.BlockSpec((None, 1, dh), lambda bi, h: (h, 0, 0)),
        ],
        out_specs=pl.BlockSpec((None, t, dh), lambda bi, h: (bi, 0, h)),
        out_shape=jax.ShapeDtypeStruct((b, t, nh * dh), bf16),
        scratch_shapes=[pltpu.VMEM((nb, dh), f32), pltpu.VMEM((2, blk, t), f32), pltpu.VMEM((t, dh + BIAS_COLS), bf16),
                        pltpu.VMEM((t, 2 * dh), bf16)],
        compiler_params=_cparams("parallel", "parallel"),
        name="moba_attention",
    )(z, z, z, kx, slope_rows, norm_g.reshape(nh, 1, dh))


def _router_logits_t(w_t32, h32, hb):
    h_lo = (h32 - hb.astype(f32)).astype(bf16)
    w_hi = w_t32.astype(bf16)
    w_lo = (w_t32 - w_hi.astype(f32)).astype(bf16)
    return _dot_nt(w_hi, hb) + _dot_nt(w_hi, h_lo) + _dot_nt(w_lo, hb)


def _top2_t(logits):
    ne = logits.shape[0]
    eid = lax.broadcasted_iota(jnp.int32, logits.shape, 0)
    m1 = jnp.max(logits, axis=0, keepdims=True)
    i1 = jnp.min(jnp.where(logits == m1, eid, ne), axis=0, keepdims=True)
    rest = jnp.where(eid == i1, -jnp.inf, logits)
    m2 = jnp.max(rest, axis=0, keepdims=True)
    i2 = jnp.min(jnp.where(rest == m2, eid, ne), axis=0, keepdims=True)
    e2 = jnp.exp(m2 - m1)
    p1 = 1.0 / (1.0 + e2)
    return i1, i2, p1, e2 * p1


def _outproj_kernel(yc_ref, ym_ref, ya_ref, w_ref, x_ref, gt_ref, g_ref, sh_ref, sc_ref, *rest, routed):
    if routed:
        wr_ref, br_ref, xo_ref, h_ref, idx_ref, p_ref = rest
    else:
        xo_ref, h_ref = rest
    c1 = yc_ref.shape[-1]
    c2 = ym_ref.shape[-1]
    acc = jnp.dot(yc_ref[...], w_ref[0:c1, :], preferred_element_type=f32)
    acc = acc + jnp.dot(ym_ref[...], w_ref[c1:c1 + c2, :], preferred_element_type=f32)
    acc = acc + jnp.dot(ya_ref[...], w_ref[c1 + c2:, :], preferred_element_type=f32)
    xn = x_ref[...] + gt_ref[...] * acc
    xo_ref[...] = xn
    h = _norm_modulate(xn, g_ref[...], sh_ref[...], sc_ref[...])
    hb = h.astype(bf16)
    h_ref[...] = h if routed else hb
    if routed:
        i1, i2, p1, p2 = _top2_t(_router_logits_t(wr_ref[...], h, hb) + br_ref[...])
        first = lax.broadcasted_iota(jnp.int32, idx_ref.shape, 0) == 0
        idx_ref[...] = jnp.where(first, i1, i2)
        p_ref[...] = jnp.where(first, p1, p2)


def _out_projection(y_conv, y_ml, y_att, w_out, x, mod4, layer, g_ffn, router=None, tm=512):
    b, t, d = x.shape
    c1, c2, c3 = y_conv.shape[-1], y_ml.shape[-1], y_att.shape[-1]
    assert c1 + c2 + c3 == w_out.shape[0]
    row = lambda c: pl.BlockSpec((None, tm, c), lambda bi, i: (bi, i, 0))
    modspec = lambda chunk: pl.BlockSpec((None, None, 1, d), lambda bi, i: (layer, bi, 0, chunk))
    in_specs = [
        row(c1), row(c2), row(c3),
        pl.BlockSpec(w_out.shape, lambda bi, i: (0, 0)),
        row(d), modspec(2),
        pl.BlockSpec((1, d), lambda bi, i: (0, 0)), modspec(3), modspec(4),
    ]
    args = [y_conv, y_ml, y_att, w_out, x, mod4, g_ffn.reshape(1, d), mod4, mod4]
    out_specs = [row(d), row(d)]
    out_shape = [jax.ShapeDtypeStruct((b, t, d), f32), jax.ShapeDtypeStruct((b, t, d), bf16 if router is None else f32)]
    if router is not None:
        w_r, b_r = router
        ne = w_r.shape[1]
        in_specs += [pl.BlockSpec((ne, d), lambda bi, i: (0, 0)), pl.BlockSpec((ne, 1), lambda bi, i: (0, 0))]
        args += [w_r.T, b_r.reshape(ne, 1)]
        kspec = pl.BlockSpec((None, TOP_K, tm), lambda bi, i: (bi, 0, i))
        out_specs += [kspec, kspec]
        out_shape += [jax.ShapeDtypeStruct((b, TOP_K, t), jnp.int32), jax.ShapeDtypeStruct((b, TOP_K, t), f32)]
    return pl.pallas_call(
        functools.partial(_outproj_kernel, routed=router is not None),
        grid=(b, t // tm),
        in_specs=in_specs,
        out_specs=out_specs,
        out_shape=out_shape,
        compiler_params=_cparams("parallel", "parallel"),
        name="out_projection",
    )(*args)


def _swiglu_rows(hb, wg_ref, wu_ref, wd_ref):
    g = jnp.dot(hb, wg_ref[...].astype(bf16), preferred_element_type=f32)
    u = jnp.dot(hb, wu_ref[...].astype(bf16), preferred_element_type=f32)
    a = (_silu(g) * u).astype(bf16)
    return jnp.dot(a, wd_ref[...].astype(bf16), preferred_element_type=f32)


def _dense_swiglu_kernel(h_ref, wg_ref, wu_ref, wd_ref, o_ref, acc_ref):
    f = pl.program_id(1)

    @pl.when(f == 0)
    def _():
        acc_ref[...] = jnp.zeros_like(acc_ref)

    acc_ref[...] += _swiglu_rows(h_ref[...], wg_ref, wu_ref, wd_ref)

    @pl.when(f == pl.num_programs(1) - 1)
    def _():
        o_ref[...] = acc_ref[...].astype(o_ref.dtype)


def _dense_swiglu(h, wg, wu, wd, layer, tm=1024, tf=256):
    n, d = h.shape
    ff = wg.shape[-1]
    return pl.pallas_call(
        _dense_swiglu_kernel,
        grid=(n // tm, ff // tf),
        in_specs=[
            pl.BlockSpec((tm, d), lambda i, f: (i, 0)),
            pl.BlockSpec((None, d, tf), lambda i, f: (layer, 0, f)),
            pl.BlockSpec((None, d, tf), lambda i, f: (layer, 0, f)),
            pl.BlockSpec((None, tf, d), lambda i, f: (layer, f, 0)),
        ],
        out_specs=pl.BlockSpec((tm, d), lambda i, f: (i, 0)),
        out_shape=jax.ShapeDtypeStruct((n, d), bf16),
        scratch_shapes=[pltpu.VMEM((tm, d), f32)],
        compiler_params=_cparams("parallel", "arbitrary"),
        name="dense_swiglu",
    )(h, wg, wu, wd)


QUARTER = 256


def _moe_swiglu_kernel(te_ref, nq_ref, tok_ref, h_hbm, wg_ref, wu_ref, wd_ref, o_ref,
                       hrows_ref, hb_ref, acc_ref, sem):
    i = pl.program_id(0)
    f = pl.program_id(1)
    nq = nq_ref[i]
    tm = acc_ref.shape[0]

    def row_copy(src_row, dst_row, rows=1):
        return pltpu.make_async_copy(h_hbm.at[pl.ds(src_row, rows)], hrows_ref.at[pl.ds(dst_row, rows)], sem)

    @pl.when(f == 0)
    def _():
        acc_ref[...] = jnp.zeros_like(acc_ref)

        @pl.when(nq > 0)
        def _():
            nrows = nq * QUARTER

            def issue(r, _):
                row_copy(tok_ref[0, r], r).start()
                return 0

            lax.fori_loop(0, nrows, issue, 0)
            row_copy(0, 0, nrows).wait()
            for q in range(tm // QUARTER):
                @pl.when(q < nq)
                def _(q=q):
                    qs = slice(q * QUARTER, (q + 1) * QUARTER)
                    hb_ref[qs, :] = hrows_ref[qs, :].astype(bf16)

    for q in range(1, tm // QUARTER + 1):
        @pl.when(nq == q)
        def _(q=q):
            rows = q * QUARTER
            acc_ref[0:rows, :] += _swiglu_rows(hb_ref[0:rows, :], wg_ref, wu_ref, wd_ref)

    @pl.when(f == pl.num_programs(1) - 1)
    def _():
        o_ref[...] = acc_ref[...].astype(o_ref.dtype)


def _moe_swiglu(h32, row_token, tile_expert, tile_quarters, wg, wu, wd, tm=1024, tf=256):
    n, d = h32.shape
    nt = row_token.shape[0]
    ff = wg.shape[-1]
    nf = ff // tf
    assert tm % QUARTER == 0

    def wmap(i, f, te, nq):
        return (te[i], 0, jnp.where(nq[i] > 0, f, nf - 1))

    def wdmap(i, f, te, nq):
        return (te[i], jnp.where(nq[i] > 0, f, nf - 1), 0)

    grid_spec = pltpu.PrefetchScalarGridSpec(
        num_scalar_prefetch=2,
        grid=(nt, nf),
        in_specs=[
            pl.BlockSpec((None, 1, tm), lambda i, f, te, nq: (i, 0, 0), memory_space=pltpu.SMEM),
            pl.BlockSpec(memory_space=pl.ANY),
            pl.BlockSpec((None, d, tf), wmap),
            pl.BlockSpec((None, d, tf), wmap),
            pl.BlockSpec((None, tf, d), wdmap),
        ],
        out_specs=pl.BlockSpec((tm, d), lambda i, f, te, nq: (i, 0)),
        scratch_shapes=[
            pltpu.VMEM((tm, d), f32),
            pltpu.VMEM((tm, d), bf16),
            pltpu.VMEM((tm, d), f32),
            pltpu.SemaphoreType.DMA,
        ],
    )
    return pl.pallas_call(
        _moe_swiglu_kernel,
        grid_spec=grid_spec,
        out_shape=jax.ShapeDtypeStruct((nt * tm, d), bf16),
        compiler_params=_cparams("arbitrary", "arbitrary"),
        name="moe_swiglu",
    )(tile_expert, tile_quarters, row_token, h32, wg, wu, wd)


def _combine_kernel(*refs, n_y, weighted, last):
    x_ref, gt_ref = refs[0], refs[1]
    ys = [r[...].astype(f32) for r in refs[2:2 + n_y]]
    rest = refs[2 + n_y:]
    if weighted:
        p = rest[0][...]
        rest = rest[1:]
        ys = [p[:, k:k + 1] * y for k, y in enumerate(ys)]
    y = ys[0]
    for yk in ys[1:]:
        y = y + yk
    xn = x_ref[...] + gt_ref[...] * y
    if last:
        g_ref, o_ref = rest
        o_ref[...] = xn * lax.rsqrt(jnp.mean(xn * xn, axis=-1, keepdims=True) + EPS) * g_ref[...]
    else:
        g_ref, sh_ref, sc_ref, xo_ref, h_ref = rest
        xo_ref[...] = xn
        h_ref[...] = _norm_modulate(xn, g_ref[...], sh_ref[...], sc_ref[...]).astype(h_ref.dtype)


def _combine(x, mod4, layer, ys, probs, g_next, last, tm=512):
    b, t, d = x.shape
    spec = pl.BlockSpec((None, tm, d), lambda bi, i: (bi, i, 0))
    modspec = lambda lyr, chunk: pl.BlockSpec((None, None, 1, d), lambda bi, i: (lyr, bi, 0, chunk))
    in_specs = [spec, modspec(layer, 5)] + [spec] * len(ys)
    args = [x, mod4, *ys]
    if probs is not None:
        in_specs.append(pl.BlockSpec((None, tm, len(ys)), lambda bi, i: (bi, i, 0)))
        args.append(probs)
    in_specs.append(pl.BlockSpec((1, d), lambda bi, i: (0, 0)))
    args.append(g_next.reshape(1, d))
    if last:
        out_specs, out_shape = spec, jax.ShapeDtypeStruct((b, t, d), f32)
    else:
        in_specs += [modspec(layer + 1, 0), modspec(layer + 1, 1)]
        args += [mod4, mod4]
        out_specs = [spec, spec]
        out_shape = [jax.ShapeDtypeStruct((b, t, d), f32), jax.ShapeDtypeStruct((b, t, d), bf16)]
    return pl.pallas_call(
        functools.partial(_combine_kernel, n_y=len(ys), weighted=probs is not None, last=last),
        grid=(b, t // tm),
        in_specs=in_specs,
        out_specs=out_specs,
        out_shape=out_shape,
        compiler_params=_cparams("parallel", "parallel"),
        name="ffn_combine",
    )(*args)


def _routing_tables(idx, n_experts, tm):
    n = idx.shape[0]
    na = n * TOP_K
    nt = na // tm + n_experts
    e_flat = idx.reshape(na)
    onehot = (e_flat[:, None] == jnp.arange(n_experts, dtype=jnp.int32)[None, :]).astype(jnp.int32)
    counts = jnp.sum(onehot, axis=0)
    rank = jnp.sum((jnp.cumsum(onehot, axis=0) - 1) * onehot, axis=1)
    tiles_per = (counts + tm - 1) // tm
    tile_end = jnp.cumsum(tiles_per)
    tile_start = tile_end - tiles_per
    slot = tile_start[e_flat] * tm + rank
    n_valid = tile_end[-1:].astype(jnp.int32)
    tile_ids = jnp.arange(nt, dtype=jnp.int32)
    tile_expert = jnp.sum((tile_ids[:, None] >= tile_end[None, :]).astype(jnp.int32), axis=1)
    last_expert = jnp.sum((n_valid - 1 >= tile_end).astype(jnp.int32))
    tile_expert = jnp.where(tile_ids < n_valid, tile_expert, last_expert).astype(jnp.int32)
    tile_rows = jnp.clip(counts[tile_expert] - (tile_ids - tile_start[tile_expert]) * tm, 0, tm)
    tile_quarters = jnp.where(tile_ids < n_valid, (tile_rows + QUARTER - 1) // QUARTER, 0).astype(jnp.int32)
    token = jnp.arange(na, dtype=jnp.int32) // TOP_K
    row_token = jnp.zeros((nt * tm,), jnp.int32).at[slot].set(token)
    return slot.reshape(n, TOP_K), row_token.reshape(nt, 1, tm), tile_expert, tile_quarters


def kernel(x, c, w_mod, b_mod, g_mix, g_ffn, w_in, conv_w, conv_b, conv_ln_g, conv_ln_b, ml_conv_w, ml_conv_b, ml_b_i, ml_b_f, ml_norm_g, attn_norm_g, w_out, ffn_w_gate, ffn_w_up, ffn_w_down, moe_w_router, moe_b_router, moe_w_gate, moe_w_up, moe_w_down, g_final):
    b, t, d = x.shape
    depth = w_mod.shape[0]
    conv_ch = conv_w.shape[-1]
    ml_dim = ml_norm_g.shape[-1]
    ml_heads = ml_b_i.shape[-1]
    n_experts = moe_w_router.shape[-1]
    att_dim = attn_norm_g.shape[-1]
    n_tok = b * t
    tm_ffn = 1024

    mod = _adaln_mod(c, w_mod, b_mod)
    mod4 = mod.reshape(depth, b, 1, 6 * d)

    gate0 = 2 * conv_ch + 4 * ml_dim
    gate1 = gate0 + 2 * ml_heads
    att_col_scale = jnp.concatenate([jnp.full((att_dim,), HEAD_DIM ** -0.5, f32), jnp.ones((2 * att_dim,), f32)])

    h = _prep(x, g_mix[0], mod4, 0, 0)
    out = None
    for l in range(depth):
        w_att = w_in[l, :, gate1:] * att_col_scale[None, :]
        w_gate = w_in[l, :, gate0:gate1]
        z, zg = _in_projection(h.reshape(n_tok, d), w_in, l, gate0, w_att, w_gate)
        z = z.reshape(b, t, -1)
        zg = zg.reshape(b, t, -1)
        zg_t = jnp.swapaxes(zg, 1, 2)

        y_conv = _conv_module(z, conv_w[l], conv_b[l], conv_ln_g[l], conv_ln_b[l])
        y_ml = _mlstm(z, zg, zg_t, (2 * conv_ch) // ml_dim, ml_conv_w[l], ml_conv_b[l], ml_b_i[l], ml_b_f[l], ml_norm_g[l])
        y_att = _moba(z, gate0 // HEAD_DIM, attn_norm_g[l])

        j = l // 2
        w_o = w_out[l].astype(bf16)
        if l % 2 == 0:
            x, h = _out_projection(y_conv, y_ml, y_att, w_o, x, mod4, l, g_ffn[l])
            y = _dense_swiglu(h.reshape(n_tok, d), ffn_w_gate, ffn_w_up, ffn_w_down, j, tm=tm_ffn)
            ys, y_probs = [y.reshape(b, t, d)], None
        else:
            x, h32, idx, probs = _out_projection(y_conv, y_ml, y_att, w_o, x, mod4, l, g_ffn[l],
                                                 router=(moe_w_router[j], moe_b_router[j]))
            slot, row_token, tile_expert, tile_quarters = _routing_tables(
                jnp.swapaxes(idx, 1, 2).reshape(n_tok, TOP_K), n_experts, tm_ffn)
            ff = moe_w_gate.shape[-1]
            y_sorted = _moe_swiglu(
                h32.reshape(n_tok, d), row_token, tile_expert + j * n_experts, tile_quarters,
                moe_w_gate.reshape(-1, d, ff), moe_w_up.reshape(-1, d, ff), moe_w_down.reshape(-1, ff, d), tm=tm_ffn)
            ys = [jnp.take(y_sorted, slot[:, k], axis=0).reshape(b, t, d) for k in range(TOP_K)]
            y_probs = jnp.swapaxes(probs, 1, 2)

        if l + 1 < depth:
            x, h = _combine(x, mod4, l, ys, y_probs, g_mix[l + 1], last=False)
        else:
            out = _combine(x, mod4, l, ys, y_probs, g_final, last=True)
    return out
```

```python
import functools

import jax
import jax.numpy as jnp
from jax import lax
from jax.experimental import pallas as pl
from jax.experimental.pallas import tpu as pltpu

f32 = jnp.float32
bf16 = jnp.bfloat16
HIGHEST = lax.Precision.HIGHEST

CONV_GROUPS = 4
HEAD_DIM = 128
ML_CHUNK = 128
MOBA_BLOCK = 256
MOBA_TOPK = 3
TOP_K = 2
EPS = 1e-6
NEG = -1e30

VMEM_LIMIT_BYTES = 56 * 1024 * 1024
LANE = 128


def _cparams(*sem):
    return pltpu.CompilerParams(dimension_semantics=sem, vmem_limit_bytes=VMEM_LIMIT_BYTES)


def _sigmoid(v):
    return 1.0 / (1.0 + jnp.exp(-v))


def _silu(v):
    return v * _sigmoid(v)


def _dot_nt(a, b, **kw):
    return lax.dot_general(a, b, (((1,), (1,)), ((), ())), preferred_element_type=f32, **kw)


def _dot_tn(a, b, **kw):
    return lax.dot_general(a, b, (((0,), (0,)), ((), ())), preferred_element_type=f32, **kw)


def _mod_kernel(c_ref, w_ref, b_ref, o_ref):
    cond = _silu(c_ref[...]).astype(bf16)
    o_ref[...] = jnp.dot(cond, w_ref[...].astype(bf16), preferred_element_type=f32) + b_ref[...]


def _adaln_mod(c, w_mod, b_mod, tn=1024):
    depth, d, n6 = w_mod.shape
    b = c.shape[0]
    return pl.pallas_call(
        _mod_kernel,
        grid=(depth, n6 // tn),
        in_specs=[
            pl.BlockSpec((b, d), lambda l, n: (0, 0)),
            pl.BlockSpec((None, d, tn), lambda l, n: (l, 0, n)),
            pl.BlockSpec((None, 1, tn), lambda l, n: (l, 0, n)),
        ],
        out_specs=pl.BlockSpec((None, b, tn), lambda l, n: (l, 0, n)),
        out_shape=jax.ShapeDtypeStruct((depth, b, n6), f32),
        compiler_params=_cparams("parallel", "parallel"),
        name="adaln_mod",
    )(c, w_mod, b_mod.reshape(depth, 1, n6))


def _norm_modulate(x, g, sh, sc):
    ms = jnp.mean(x * x, axis=-1, keepdims=True)
    h = x * lax.rsqrt(ms + EPS) * g
    return h * (1.0 + sc) + sh


def _prep_kernel(x_ref, g_ref, sh_ref, sc_ref, h_ref):
    h_ref[...] = _norm_modulate(x_ref[...], g_ref[...], sh_ref[...], sc_ref[...]).astype(h_ref.dtype)


def _prep(x, g, mod4, layer, chunk, tm=512):
    b, t, d = x.shape
    return pl.pallas_call(
        _prep_kernel,
        grid=(b, t // tm),
        in_specs=[
            pl.BlockSpec((None, tm, d), lambda bi, i: (bi, i, 0)),
            pl.BlockSpec((1, d), lambda bi, i: (0, 0)),
            pl.BlockSpec((None, None, 1, d), lambda bi, i: (layer, bi, 0, chunk)),
            pl.BlockSpec((None, None, 1, d), lambda bi, i: (layer, bi, 0, chunk + 1)),
        ],
        out_specs=pl.BlockSpec((None, tm, d), lambda bi, i: (bi, i, 0)),
        out_shape=jax.ShapeDtypeStruct((b, t, d), bf16),
        compiler_params=_cparams("parallel", "parallel"),
        name="prep",
    )(x, g.reshape(1, d), mod4, mod4)


def _inproj_kernel(h_ref, wa_ref, wb_ref, wg_ref, z_ref, zg_ref, *, n_a):
    j = pl.program_id(1)

    @pl.when(j == 0)
    def _():
        zg_ref[...] = jnp.dot(h_ref[...], wg_ref[...].astype(bf16), preferred_element_type=f32)

    @pl.when(j < n_a)
    def _():
        z_ref[...] = jnp.dot(h_ref[...], wa_ref[...].astype(bf16), preferred_element_type=f32).astype(bf16)

    @pl.when(j >= n_a)
    def _():
        z_ref[...] = jnp.dot(h_ref[...], wb_ref[...].astype(bf16), preferred_element_type=f32).astype(bf16)


def _in_projection(h, w_a, w_b, w_gate, layer, tm=2048, tn=512):
    n, d = h.shape
    n_a = w_a.shape[-1] // tn
    n_b = w_b.shape[-1] // tn
    ng = w_gate.shape[-1]
    assert n_a * tn == w_a.shape[-1] and n_b * tn == w_b.shape[-1]
    return pl.pallas_call(
        functools.partial(_inproj_kernel, n_a=n_a),
        grid=(n // tm, n_a + n_b),
        in_specs=[
            pl.BlockSpec((tm, d), lambda i, j: (i, 0)),
            pl.BlockSpec((None, d, tn), lambda i, j: (layer, 0, jnp.minimum(j, n_a - 1))),
            pl.BlockSpec((None, d, tn), lambda i, j: (layer, 0, jnp.maximum(j - n_a, 0))),
            pl.BlockSpec((None, d, ng), lambda i, j: (layer, 0, 0)),
        ],
        out_specs=[
            pl.BlockSpec((tm, tn), lambda i, j: (i, j)),
            pl.BlockSpec((tm, ng), lambda i, j: (i, 0)),
        ],
        out_shape=[
            jax.ShapeDtypeStruct((n, (n_a + n_b) * tn), bf16),
            jax.ShapeDtypeStruct((n, ng), f32),
        ],
        compiler_params=_cparams("parallel", "arbitrary"),
        name="in_projection",
    )(h, w_a, w_b, w_gate)


CONV_PAD = 32
ROWS = 128


def _conv_kernel(ag_ref, w_ref, b_ref, lg_ref, lb_ref, o_ref, xp_ref):
    t, c = o_ref.shape
    taps = w_ref.shape[0]
    gc = c // CONV_GROUPS
    for cg in range(CONV_GROUPS):
        cs = slice(cg * gc, (cg + 1) * gc)
        gs = slice(c + cg * gc, c + (cg + 1) * gc)
        xp_ref[cg, 0:CONV_PAD, :] = jnp.zeros((CONV_PAD, gc), f32)

        def glu(i, _, cg=cg, cs=cs, gs=gs):
            r0 = pl.multiple_of(i * ROWS, ROWS)
            a = ag_ref[pl.ds(r0, ROWS), cs].astype(f32)
            g = ag_ref[pl.ds(r0, ROWS), gs].astype(f32)
            xp_ref[cg, pl.ds(CONV_PAD + r0, ROWS), :] = a * _sigmoid(g)
            return 0

        lax.fori_loop(0, t // ROWS, glu, 0)

        def body(i, _, cg=cg, cs=cs):
            r0 = pl.multiple_of(i * ROWS, ROWS)
            acc = jnp.zeros((ROWS, gc), f32) + b_ref[:, cs]
            for k in range(taps):
                acc = acc + xp_ref[cg, pl.ds(r0 + (CONV_PAD - (taps - 1) + k), ROWS), :] * w_ref[k:k + 1, cs]
            mu = jnp.mean(acc, axis=-1, keepdims=True)
            dv = acc - mu
            var = jnp.mean(dv * dv, axis=-1, keepdims=True)
            y = dv * lax.rsqrt(var + EPS) * lg_ref[:, cs] + lb_ref[:, cs]
            o_ref[pl.ds(r0, ROWS), cs] = _silu(y).astype(o_ref.dtype)
            return 0

        lax.fori_loop(0, t // ROWS, body, 0, unroll=2)


def _conv_module(z, conv_w, conv_b, ln_g, ln_b):
    b, t, _ = z.shape
    taps, c = conv_w.shape
    assert taps - 1 <= CONV_PAD and c // CONV_GROUPS == LANE
    return pl.pallas_call(
        _conv_kernel,
        grid=(b,),
        in_specs=[
            pl.BlockSpec((None, t, 2 * c), lambda bi: (bi, 0, 0)),
            pl.BlockSpec((taps, c), lambda bi: (0, 0)),
            pl.BlockSpec((1, c), lambda bi: (0, 0)),
            pl.BlockSpec((1, c), lambda bi: (0, 0)),
            pl.BlockSpec((1, c), lambda bi: (0, 0)),
        ],
        out_specs=pl.BlockSpec((None, t, c), lambda bi: (bi, 0, 0)),
        out_shape=jax.ShapeDtypeStruct((b, t, c), bf16),
        scratch_shapes=[pltpu.VMEM((CONV_GROUPS, t + CONV_PAD, c // CONV_GROUPS), f32)],
        compiler_params=_cparams("parallel"),
        name="conv_module",
    )(z, conv_w, conv_b.reshape(1, c), ln_g.reshape(1, c), ln_b.reshape(1, c))


ML_PAD = 8


def _mlstm_kernel(q_ref, k_ref, v_ref, o_ref, gr_ref, cw_ref, cb_ref, bic_ref, bfc_ref,
                  ng_ref, y_ref, qp_ref, kp_ref, qs_ref, ks_ref, va_ref, cta_ref, m_ref):
    t, dm = q_ref.shape
    nh = dm // HEAD_DIM
    L = ML_CHUNK
    nc = t // L
    taps = cw_ref.shape[0]
    kscale = HEAD_DIM ** -0.5

    for h in range(nh):
        hs = slice(h * HEAD_DIM, (h + 1) * HEAD_DIM)
        ks_cols = slice(dm + h * HEAD_DIM, dm + (h + 1) * HEAD_DIM)
        qp_ref[h, 0:ML_PAD, :] = jnp.zeros((ML_PAD, HEAD_DIM), f32)
        kp_ref[h, 0:ML_PAD, :] = jnp.zeros((ML_PAD, HEAD_DIM), f32)

        def stage(i, _, h=h, hs=hs):
            r0 = pl.multiple_of(i * ROWS, ROWS)
            qp_ref[h, pl.ds(ML_PAD + r0, ROWS), :] = q_ref[pl.ds(r0, ROWS), hs].astype(f32)
            kp_ref[h, pl.ds(ML_PAD + r0, ROWS), :] = k_ref[pl.ds(r0, ROWS), hs].astype(f32)
            return 0

        lax.fori_loop(0, t // ROWS, stage, 0)

        def sconv(i, _, h=h, hs=hs, ks_cols=ks_cols):
            r0 = pl.multiple_of(i * ROWS, ROWS)
            aq = jnp.zeros((ROWS, HEAD_DIM), f32) + cb_ref[:, hs]
            ak = jnp.zeros((ROWS, HEAD_DIM), f32) + cb_ref[:, ks_cols]
            for j in range(taps):
                off = ML_PAD - (taps - 1) + j
                aq = aq + qp_ref[h, pl.ds(r0 + off, ROWS), :] * cw_ref[j:j + 1, hs]
                ak = ak + kp_ref[h, pl.ds(r0 + off, ROWS), :] * cw_ref[j:j + 1, ks_cols]
            qs_ref[pl.ds(r0, ROWS), hs] = _silu(aq).astype(bf16)
            ks_ref[pl.ds(r0, ROWS), hs] = (_silu(ak) * kscale).astype(bf16)
            return 0

        lax.fori_loop(0, t // ROWS, sconv, 0)

    for h in range(nh):
        hs = slice(h * HEAD_DIM, (h + 1) * HEAD_DIM)
        va_ref[:, 2 * h * HEAD_DIM:(2 * h + 1) * HEAD_DIM] = v_ref[:, hs]
        va_ref[:, (2 * h + 1) * HEAD_DIM:(2 * h + 2) * HEAD_DIM] = jnp.ones((t, HEAD_DIM), bf16)

    ri = lax.broadcasted_iota(jnp.int32, (L, L), 0)
    ci = lax.broadcasted_iota(jnp.int32, (L, L), 1)
    causal = ri >= ci
    lower_f = causal.astype(f32)
    eye_f = (ri == ci).astype(f32)
    eye_b = eye_f.astype(bf16)
    upper_b = (ri <= ci).astype(bf16)
    ones_b = jnp.ones((L, HEAD_DIM), bf16)
    lane = lax.broadcasted_iota(jnp.int32, (nh, L), 1)

    def log_sigmoid(v):
        return jnp.minimum(v, 0.0) - jnp.log(1.0 + jnp.exp(-jnp.abs(v)))

    def split(a, pieces):
        out = []
        for _ in range(pieces - 1):
            p = a.astype(bf16)
            out.append(p)
            a = a - p.astype(f32)
        out.append(a.astype(bf16))
        return out

    def rows_to_columns(weighted):
        return sum(jnp.dot(p, ones_b, preferred_element_type=f32) for p in split(weighted, 2))

    cta_ref[...] = jnp.zeros_like(cta_ref)
    m_ref[...] = jnp.zeros_like(m_ref)

    def chunk(i, _):
        r0 = pl.multiple_of(i * L, L)
        li = gr_ref[0:nh, pl.ds(r0, L)] + bic_ref[...]
        lf = log_sigmoid(gr_ref[nh:2 * nh, pl.ds(r0, L)] + bfc_ref[...])
        b = sum(jnp.dot(p, upper_b, preferred_element_type=f32) for p in split(lf, 3))
        g = li - b
        cmax = g
        for sh in (1, 2, 4, 8, 16, 32, 64):
            cmax = jnp.maximum(cmax, jnp.where(lane >= sh, pltpu.roll(cmax, sh, axis=1), NEG))
        m_prev = m_ref[...]
        big_m = jnp.maximum(cmax, m_prev)
        m_last = big_m[:, L - 1:L]
        b_last = b[:, L - 1:L]
        wk = jnp.exp(g - m_last)
        decay = jnp.exp(m_prev[:, 0:1] - m_last)
        m_ref[...] = jnp.broadcast_to(b_last + m_last, (nh, L))

        heads = range(nh)
        hsl = [slice(h * HEAD_DIM, (h + 1) * HEAD_DIM) for h in heads]
        q = [qs_ref[pl.ds(r0, L), hsl[h]] for h in heads]
        k = [ks_ref[pl.ds(r0, L), hsl[h]] for h in heads]
        va = [va_ref[pl.ds(r0, L), 2 * h * HEAD_DIM:(2 * h + 2) * HEAD_DIM] for h in heads]
        cta = [cta_ref[h] for h in heads]
        m_col = [rows_to_columns(eye_f * big_m[h:h + 1, :]) for h in heads]
        b_col = [rows_to_columns(lower_f * lf[h:h + 1, :]) for h in heads]
        qk = [_dot_nt(q[h], k[h]) for h in heads]
        k_t = [_dot_nt(eye_b, k[h]) for h in heads]
        inter = [jnp.dot(q[h], cta[h].astype(bf16), preferred_element_type=f32) for h in heads]
        s = [(qk[h] * jnp.exp(jnp.where(causal, g[h:h + 1, :] - m_col[h], NEG))).astype(bf16) for h in heads]
        kw_t = [(k_t[h] * wk[h:h + 1, :]).astype(bf16) for h in heads]
        w_inter = [jnp.exp(m_prev[h:h + 1, 0:1] - m_col[h]) for h in heads]
        both = [jnp.dot(s[h], va[h], preferred_element_type=f32)
                + jnp.concatenate([w_inter[h], w_inter[h]], axis=1) * inter[h] for h in heads]
        new_cta = [decay[h:h + 1, :] * cta[h] + jnp.dot(kw_t[h], va[h], preferred_element_type=f32) for h in heads]
        hout = [both[h][:, 0:HEAD_DIM] / jnp.maximum(jnp.abs(both[h][:, HEAD_DIM:2 * HEAD_DIM]),
                                                     jnp.exp(-(b_col[h] + m_col[h]))) for h in heads]
        ssq = [sum(jnp.dot(pc, ones_b, preferred_element_type=f32) for pc in split(hout[h] * hout[h], 2)) for h in heads]
        for h in heads:
            hn = hout[h] * lax.rsqrt(ssq[h] * (1.0 / HEAD_DIM) + EPS) * ng_ref[:, hsl[h]]
            og = _sigmoid(o_ref[pl.ds(r0, L), hsl[h]].astype(f32))
            y_ref[pl.ds(r0, L), hsl[h]] = (og * hn).astype(y_ref.dtype)
            cta_ref[h] = new_cta[h]
        return 0

    lax.fori_loop(0, nc, chunk, 0, unroll=2)


def _mlstm(z, zg_t, col0, ml_conv_w, ml_conv_b, b_i, b_f, norm_g):
    b, t, _ = z.shape
    dm = norm_g.shape[0]
    nh = dm // HEAD_DIM
    taps = ml_conv_w.shape[0]
    assert taps - 1 <= ML_PAD

    def zspec(j):
        return pl.BlockSpec((None, t, dm), lambda bi: (bi, 0, col0 + j))

    return pl.pallas_call(
        _mlstm_kernel,
        grid=(b,),
        in_specs=[
            zspec(0), zspec(1), zspec(2), zspec(3),
            pl.BlockSpec((None, 2 * nh, t), lambda bi: (bi, 0, 0)),
            pl.BlockSpec((taps, 2 * dm), lambda bi: (0, 0)),
            pl.BlockSpec((1, 2 * dm), lambda bi: (0, 0)),
            pl.BlockSpec((nh, 1), lambda bi: (0, 0)),
            pl.BlockSpec((nh, 1), lambda bi: (0, 0)),
            pl.BlockSpec((1, dm), lambda bi: (0, 0)),
        ],
        out_specs=pl.BlockSpec((None, t, dm), lambda bi: (bi, 0, 0)),
        out_shape=jax.ShapeDtypeStruct((b, t, dm), bf16),
        scratch_shapes=[
            pltpu.VMEM((nh, t + ML_PAD, HEAD_DIM), f32),
            pltpu.VMEM((nh, t + ML_PAD, HEAD_DIM), f32),
            pltpu.VMEM((t, dm), bf16),
            pltpu.VMEM((t, dm), bf16),
            pltpu.VMEM((t, 2 * dm), bf16),
            pltpu.VMEM((nh, HEAD_DIM, 2 * HEAD_DIM), f32),
            pltpu.VMEM((nh, ML_CHUNK), f32),
        ],
        compiler_params=_cparams("parallel"),
        name="mlstm",
    )(z, z, z, z, zg_t, ml_conv_w, ml_conv_b.reshape(1, 2 * dm), b_i.reshape(nh, 1), b_f.reshape(nh, 1),
      norm_g.reshape(1, dm))


MASK_BIG = 1e30
BIAS_COLS = 128


def _moba_kernel(q_ref, k_ref, v_ref, kx_ref, sl_ref, ng_ref, y_ref, km_ref, s_ref, kc_ref, vc_ref):
    t, dh = q_ref.shape
    blk = MOBA_BLOCK
    nb = t // blk

    kc_ref[:, 0:dh] = k_ref[...]
    kc_ref[:, dh:dh + BIAS_COLS] = kx_ref[...]
    vc_ref[:, 0:dh] = v_ref[...]
    vc_ref[:, dh:2 * dh] = jnp.ones((t, dh), bf16)

    for n in range(nb):
        km_ref[n:n + 1, :] = jnp.mean(k_ref[n * blk:(n + 1) * blk, :].astype(f32), axis=0, keepdims=True)
    gate_t = _dot_nt(km_ref[...], q_ref[...].astype(f32), precision=HIGHEST)

    blk_id = lax.broadcasted_iota(jnp.int32, (nb, blk), 0)
    ri = lax.broadcasted_iota(jnp.int32, (blk, blk), 0)
    ci = lax.broadcasted_iota(jnp.int32, (blk, blk), 1)
    causal = ri >= ci
    eye = (ri == ci).astype(bf16)
    lane = lax.broadcasted_iota(jnp.int32, (blk, BIAS_COLS), 1)
    rowf = lax.broadcasted_iota(jnp.int32, (blk, BIAS_COLS), 0).astype(f32)
    slope = sl_ref[:, 0:1]

    def logits_pass(j):
        qj = q_ref[j * blk:(j + 1) * blk, :]
        masked_t = jnp.zeros((nb, blk), f32)
        if j > 0:
            g = jnp.where(blk_id < j, gate_t[:, j * blk:(j + 1) * blk], NEG)
            for n in range(j):
                gn = g[n:n + 1, :]
                ahead = (g > gn) | ((g == gn) & (blk_id < n))
                rank = jnp.sum(jnp.where(ahead, 1.0, 0.0), axis=0, keepdims=True)
                masked_t = jnp.where((blk_id == n) & (rank >= float(MOBA_TOPK)), -1.0, masked_t)
        masked_pad = jnp.concatenate([masked_t, jnp.zeros((BIAS_COLS - nb, blk), f32)], axis=0).astype(bf16)
        qx = _dot_nt(eye, masked_pad)
        qx = jnp.where(lane == nb, -slope * float(blk * j), qx)
        qx = jnp.where(lane == nb + 1, -slope * rowf, qx)
        qx = jnp.where((lane == nb + 2) | (lane == nb + 3), 1.0, qx).astype(bf16)
        qc = jnp.concatenate([qj, qx], axis=1)
        mx = None
        for n in range(j + 1):
            ks = slice(n * blk, (n + 1) * blk)
            lg = _dot_nt(qc, kc_ref[ks, :])
            if n == j:
                lg = jnp.where(causal, lg, NEG)
            s_ref[j % 2, :, ks] = lg
            half = jnp.maximum(lg[:, 0:LANE], lg[:, LANE:2 * LANE])
            mx = half if mx is None else jnp.maximum(mx, half)
        return jnp.max(mx, axis=-1, keepdims=True)

    def value_pass(j, m):
        acc = jnp.zeros((blk, 2 * dh), f32)
        for n in range(j + 1):
            ks = slice(n * blk, (n + 1) * blk)
            p = jnp.exp((s_ref[j % 2, :, ks] - m).astype(bf16))
            acc = acc + jnp.dot(p, vc_ref[ks, :], preferred_element_type=f32)
        o = acc[:, 0:dh] / acc[:, dh:2 * dh]
        o = o * lax.rsqrt(jnp.mean(o * o, axis=-1, keepdims=True) + EPS) * ng_ref[...]
        y_ref[j * blk:(j + 1) * blk, :] = o.astype(y_ref.dtype)

    m_prev = logits_pass(0)
    for j in range(1, nb):
        m_cur = logits_pass(j)
        value_pass(j - 1, m_prev)
        m_prev = m_cur
    value_pass(nb - 1, m_prev)


def _moba(z, col0, norm_g):
    b, t, _ = z.shape
    dh = HEAD_DIM
    nh = norm_g.shape[0] // dh
    blk = MOBA_BLOCK
    assert t % blk == 0
    nb = t // blk
    assert nb + 4 <= BIAS_COLS
    slopes = 2.0 ** (-(8.0 / nh) * jnp.arange(1, nh + 1, dtype=f32))
    pos = jnp.arange(t, dtype=jnp.int32)
    col = jnp.arange(BIAS_COLS, dtype=jnp.int32)[None, :]
    kblk = (pos // blk)[:, None]
    base = jnp.where(col == kblk, MASK_BIG, 0.0) + jnp.where((col == nb) | (col == nb + 1), 1.0, 0.0)
    per_head = (jnp.where(col == nb + 2, (kblk * blk).astype(f32), 0.0)
                + jnp.where(col == nb + 3, (pos % blk).astype(f32)[:, None], 0.0))
    kx = (base[None] + slopes[:, None, None] * per_head[None]).astype(bf16)
    slope_rows = jnp.broadcast_to(slopes[:, None, None], (nh, 1, LANE))

    def zspec(j):
        return pl.BlockSpec((None, t, dh), lambda bi, h: (bi, 0, col0 + j * nh + h))

    return pl.pallas_call(
        _moba_kernel,
        grid=(b, nh),
        in_specs=[
            zspec(0), zspec(1), zspec(2),
            pl.BlockSpec((None, t, BIAS_COLS), lambda bi, h: (h, 0, 0)),
            pl.BlockSpec((None, 1, LANE), lambda bi, h: (h, 0, 0)),
            pl.BlockSpec((None, 1, dh), lambda bi, h: (h, 0, 0)),
        ],
        out_specs=pl.BlockSpec((None, t, dh), lambda bi, h: (bi, 0, h)),
        out_shape=jax.ShapeDtypeStruct((b, t, nh * dh), bf16),
        scratch_shapes=[pltpu.VMEM((nb, dh), f32), pltpu.VMEM((2, blk, t), f32), pltpu.VMEM((t, dh + BIAS_COLS), bf16),
                        pltpu.VMEM((t, 2 * dh), bf16)],
        compiler_params=_cparams("parallel", "parallel"),
        name="moba_attention",
    )(z, z, z, kx, slope_rows, norm_g.reshape(nh, 1, dh))


def _router_logits_t(w_t32, h32, hb):
    h_lo = (h32 - hb.astype(f32)).astype(bf16)
    w_hi = w_t32.astype(bf16)
    w_lo = (w_t32 - w_hi.astype(f32)).astype(bf16)
    return _dot_nt(w_hi, hb) + _dot_nt(w_hi, h_lo) + _dot_nt(w_lo, hb)


def _top2_t(logits):
    ne = logits.shape[0]
    eid = lax.broadcasted_iota(jnp.int32, logits.shape, 0)
    m1 = jnp.max(logits, axis=0, keepdims=True)
    i1 = jnp.min(jnp.where(logits == m1, eid, ne), axis=0, keepdims=True)
    rest = jnp.where(eid == i1, -jnp.inf, logits)
    m2 = jnp.max(rest, axis=0, keepdims=True)
    i2 = jnp.min(jnp.where(rest == m2, eid, ne), axis=0, keepdims=True)
    e2 = jnp.exp(m2 - m1)
    p1 = 1.0 / (1.0 + e2)
    return i1, i2, p1, e2 * p1


def _outproj_kernel(yc_ref, ym_ref, ya_ref, w_ref, x_ref, gt_ref, g_ref, sh_ref, sc_ref, *rest, routed):
    if routed:
        wr_ref, br_ref, xo_ref, h_ref, idx_ref, p_ref = rest
    else:
        xo_ref, h_ref = rest
    c1 = yc_ref.shape[-1]
    c2 = ym_ref.shape[-1]
    acc = jnp.dot(yc_ref[...], w_ref[0:c1, :], preferred_element_type=f32)
    acc = acc + jnp.dot(ym_ref[...], w_ref[c1:c1 + c2, :], preferred_element_type=f32)
    acc = acc + jnp.dot(ya_ref[...], w_ref[c1 + c2:, :], preferred_element_type=f32)
    xn = x_ref[...] + gt_ref[...] * acc
    xo_ref[...] = xn
    h = _norm_modulate(xn, g_ref[...], sh_ref[...], sc_ref[...])
    hb = h.astype(bf16)
    h_ref[...] = h if routed else hb
    if routed:
        i1, i2, p1, p2 = _top2_t(_router_logits_t(wr_ref[...], h, hb) + br_ref[...])
        first = lax.broadcasted_iota(jnp.int32, idx_ref.shape, 0) == 0
        idx_ref[...] = jnp.where(first, i1, i2)
        p_ref[...] = jnp.where(first, p1, p2)


def _out_projection(y_conv, y_ml, y_att, w_out, x, mod4, layer, g_ffn, router=None, tm=512):
    b, t, d = x.shape
    c1, c2, c3 = y_conv.shape[-1], y_ml.shape[-1], y_att.shape[-1]
    assert c1 + c2 + c3 == w_out.shape[0]
    row = lambda c: pl.BlockSpec((None, tm, c), lambda bi, i: (bi, i, 0))
    modspec = lambda chunk: pl.BlockSpec((None, None, 1, d), lambda bi, i: (layer, bi, 0, chunk))
    in_specs = [
        row(c1), row(c2), row(c3),
        pl.BlockSpec(w_out.shape, lambda bi, i: (0, 0)),
        row(d), modspec(2),
        pl.BlockSpec((1, d), lambda bi, i: (0, 0)), modspec(3), modspec(4),
    ]
    args = [y_conv, y_ml, y_att, w_out, x, mod4, g_ffn.reshape(1, d), mod4, mod4]
    out_specs = [row(d), row(d)]
    out_shape = [jax.ShapeDtypeStruct((b, t, d), f32), jax.ShapeDtypeStruct((b, t, d), bf16 if router is None else f32)]
    if router is not None:
        w_r, b_r = router
        ne = w_r.shape[1]
        in_specs += [pl.BlockSpec((ne, d), lambda bi, i: (0, 0)), pl.BlockSpec((ne, 1), lambda bi, i: (0, 0))]
        args += [w_r.T, b_r.reshape(ne, 1)]
        kspec = pl.BlockSpec((None, TOP_K, tm), lambda bi, i: (bi, 0, i))
        out_specs += [kspec, kspec]
        out_shape += [jax.ShapeDtypeStruct((b, TOP_K, t), jnp.int32), jax.ShapeDtypeStruct((b, TOP_K, t), f32)]
    return pl.pallas_call(
        functools.partial(_outproj_kernel, routed=router is not None),
        grid=(b, t // tm),
        in_specs=in_specs,
        out_specs=out_specs,
        out_shape=out_shape,
        compiler_params=_cparams("parallel", "parallel"),
        name="out_projection",
    )(*args)


def _swiglu_rows(hb, wg_ref, wu_ref, wd_ref):
    g = jnp.dot(hb, wg_ref[...].astype(bf16), preferred_element_type=f32)
    u = jnp.dot(hb, wu_ref[...].astype(bf16), preferred_element_type=f32)
    a = (_silu(g) * u).astype(bf16)
    return jnp.dot(a, wd_ref[...].astype(bf16), preferred_element_type=f32)


def _dense_swiglu_kernel(h_ref, wg_ref, wu_ref, wd_ref, o_ref, acc_ref):
    f = pl.program_id(1)

    @pl.when(f == 0)
    def _():
        acc_ref[...] = jnp.zeros_like(acc_ref)

    acc_ref[...] += _swiglu_rows(h_ref[...], wg_ref, wu_ref, wd_ref)

    @pl.when(f == pl.num_programs(1) - 1)
    def _():
        o_ref[...] = acc_ref[...].astype(o_ref.dtype)


def _dense_swiglu(h, wg, wu, wd, layer, tm=1024, tf=256):
    n, d = h.shape
    ff = wg.shape[-1]
    return pl.pallas_call(
        _dense_swiglu_kernel,
        grid=(n // tm, ff // tf),
        in_specs=[
            pl.BlockSpec((tm, d), lambda i, f: (i, 0)),
            pl.BlockSpec((None, d, tf), lambda i, f: (layer, 0, f)),
            pl.BlockSpec((None, d, tf), lambda i, f: (layer, 0, f)),
            pl.BlockSpec((None, tf, d), lambda i, f: (layer, f, 0)),
        ],
        out_specs=pl.BlockSpec((tm, d), lambda i, f: (i, 0)),
        out_shape=jax.ShapeDtypeStruct((n, d), bf16),
        scratch_shapes=[pltpu.VMEM((tm, d), f32)],
        compiler_params=_cparams("parallel", "arbitrary"),
        name="dense_swiglu",
    )(h, wg, wu, wd)


QUARTER = 256
GATHER_UNROLL = 8


def _moe_swiglu_kernel(te_ref, nq_ref, tok_ref, h_hbm, wg_ref, wu_ref, wd_ref, o_ref,
                       hrows_ref, hb_ref, acc_ref, sem):
    i = pl.program_id(0)
    f = pl.program_id(1)
    nq = nq_ref[i]
    tm = acc_ref.shape[0]

    def row_copy(src_row, dst_row, rows=1):
        return pltpu.make_async_copy(h_hbm.at[pl.ds(src_row, rows)], hrows_ref.at[pl.ds(dst_row, rows)], sem)

    @pl.when(f == 0)
    def _():
        acc_ref[...] = jnp.zeros_like(acc_ref)

        @pl.when(nq > 0)
        def _():
            nrows = nq * QUARTER

            def issue(g, _):
                for u in range(GATHER_UNROLL):
                    r = g * GATHER_UNROLL + u
                    row_copy(tok_ref[0, r], r).start()
                return 0

            lax.fori_loop(0, nq * (QUARTER // GATHER_UNROLL), issue, 0)
            row_copy(0, 0, nrows).wait()
            for q in range(tm // QUARTER):
                @pl.when(q < nq)
                def _(q=q):
                    qs = slice(q * QUARTER, (q + 1) * QUARTER)
                    hb_ref[qs, :] = hrows_ref[qs, :].astype(bf16)

    for q in range(1, tm // QUARTER + 1):
        @pl.when(nq == q)
        def _(q=q):
            rows = q * QUARTER
            acc_ref[0:rows, :] += _swiglu_rows(hb_ref[0:rows, :], wg_ref, wu_ref, wd_ref)

    @pl.when(f == pl.num_programs(1) - 1)
    def _():
        o_ref[...] = acc_ref[...].astype(o_ref.dtype)


def _moe_swiglu(h32, row_token, tile_expert, tile_quarters, wg, wu, wd, tm=1024, tf=256):
    n, d = h32.shape
    nt = row_token.shape[0]
    ff = wg.shape[-1]
    nf = ff // tf
    assert tm % QUARTER == 0

    def wmap(i, f, te, nq):
        return (te[i], 0, jnp.where(nq[i] > 0, f, nf - 1))

    def wdmap(i, f, te, nq):
        return (te[i], jnp.where(nq[i] > 0, f, nf - 1), 0)

    grid_spec = pltpu.PrefetchScalarGridSpec(
        num_scalar_prefetch=2,
        grid=(nt, nf),
        in_specs=[
            pl.BlockSpec((None, 1, tm), lambda i, f, te, nq: (i, 0, 0), memory_space=pltpu.SMEM),
            pl.BlockSpec(memory_space=pl.ANY),
            pl.BlockSpec((None, d, tf), wmap),
            pl.BlockSpec((None, d, tf), wmap),
            pl.BlockSpec((None, tf, d), wdmap),
        ],
        out_specs=pl.BlockSpec((tm, d), lambda i, f, te, nq: (i, 0)),
        scratch_shapes=[
            pltpu.VMEM((tm, d), f32),
            pltpu.VMEM((tm, d), bf16),
            pltpu.VMEM((tm, d), f32),
            pltpu.SemaphoreType.DMA,
        ],
    )
    return pl.pallas_call(
        _moe_swiglu_kernel,
        grid_spec=grid_spec,
        out_shape=jax.ShapeDtypeStruct((nt * tm, d), bf16),
        compiler_params=_cparams("arbitrary", "arbitrary"),
        name="moe_swiglu",
    )(tile_expert, tile_quarters, row_token, h32, wg, wu, wd)


def _combine_kernel(*refs, n_y, weighted, last):
    x_ref, gt_ref = refs[0], refs[1]
    ys = [r[...].astype(f32) for r in refs[2:2 + n_y]]
    rest = refs[2 + n_y:]
    if weighted:
        p = rest[0][...]
        rest = rest[1:]
        ys = [p[:, k:k + 1] * y for k, y in enumerate(ys)]
    y = ys[0]
    for yk in ys[1:]:
        y = y + yk
    xn = x_ref[...] + gt_ref[...] * y
    if last:
        g_ref, o_ref = rest
        o_ref[...] = xn * lax.rsqrt(jnp.mean(xn * xn, axis=-1, keepdims=True) + EPS) * g_ref[...]
    else:
        g_ref, sh_ref, sc_ref, xo_ref, h_ref = rest
        xo_ref[...] = xn
        h_ref[...] = _norm_modulate(xn, g_ref[...], sh_ref[...], sc_ref[...]).astype(h_ref.dtype)


def _combine(x, mod4, layer, ys, probs, g_next, last, tm=512):
    b, t, d = x.shape
    spec = pl.BlockSpec((None, tm, d), lambda bi, i: (bi, i, 0))
    modspec = lambda lyr, chunk: pl.BlockSpec((None, None, 1, d), lambda bi, i: (lyr, bi, 0, chunk))
    in_specs = [spec, modspec(layer, 5)] + [spec] * len(ys)
    args = [x, mod4, *ys]
    if probs is not None:
        in_specs.append(pl.BlockSpec((None, tm, len(ys)), lambda bi, i: (bi, i, 0)))
        args.append(probs)
    in_specs.append(pl.BlockSpec((1, d), lambda bi, i: (0, 0)))
    args.append(g_next.reshape(1, d))
    if last:
        out_specs, out_shape = spec, jax.ShapeDtypeStruct((b, t, d), f32)
    else:
        in_specs += [modspec(layer + 1, 0), modspec(layer + 1, 1)]
        args += [mod4, mod4]
        out_specs = [spec, spec]
        out_shape = [jax.ShapeDtypeStruct((b, t, d), f32), jax.ShapeDtypeStruct((b, t, d), bf16)]
    return pl.pallas_call(
        functools.partial(_combine_kernel, n_y=len(ys), weighted=probs is not None, last=last),
        grid=(b, t // tm),
        in_specs=in_specs,
        out_specs=out_specs,
        out_shape=out_shape,
        compiler_params=_cparams("parallel", "parallel"),
        name="ffn_combine",
    )(*args)


def _routing_tables(idx, n_experts, tm):
    n = idx.shape[0]
    na = n * TOP_K
    nt = na // tm + n_experts
    e_flat = idx.reshape(na)
    onehot = (e_flat[:, None] == jnp.arange(n_experts, dtype=jnp.int32)[None, :]).astype(jnp.int32)
    counts = jnp.sum(onehot, axis=0)
    rank = jnp.sum((jnp.cumsum(onehot, axis=0) - 1) * onehot, axis=1)
    tiles_per = (counts + tm - 1) // tm
    tile_end = jnp.cumsum(tiles_per)
    tile_start = tile_end - tiles_per
    slot = tile_start[e_flat] * tm + rank
    n_valid = tile_end[-1:].astype(jnp.int32)
    tile_ids = jnp.arange(nt, dtype=jnp.int32)
    tile_expert = jnp.sum((tile_ids[:, None] >= tile_end[None, :]).astype(jnp.int32), axis=1)
    last_expert = jnp.sum((n_valid - 1 >= tile_end).astype(jnp.int32))
    tile_expert = jnp.where(tile_ids < n_valid, tile_expert, last_expert).astype(jnp.int32)
    tile_rows = jnp.clip(counts[tile_expert] - (tile_ids - tile_start[tile_expert]) * tm, 0, tm)
    tile_quarters = jnp.where(tile_ids < n_valid, (tile_rows + QUARTER - 1) // QUARTER, 0).astype(jnp.int32)
    token = jnp.arange(na, dtype=jnp.int32) // TOP_K
    row_token = jnp.zeros((nt * tm,), jnp.int32).at[slot].set(token)
    return slot.reshape(n, TOP_K), row_token.reshape(nt, 1, tm), tile_expert, tile_quarters


def kernel(x, c, w_mod, b_mod, g_mix, g_ffn, w_in, conv_w, conv_b, conv_ln_g, conv_ln_b, ml_conv_w, ml_conv_b, ml_b_i, ml_b_f, ml_norm_g, attn_norm_g, w_out, ffn_w_gate, ffn_w_up, ffn_w_down, moe_w_router, moe_b_router, moe_w_gate, moe_w_up, moe_w_down, g_final):
    b, t, d = x.shape
    depth = w_mod.shape[0]
    conv_ch = conv_w.shape[-1]
    ml_dim = ml_norm_g.shape[-1]
    ml_heads = ml_b_i.shape[-1]
    n_experts = moe_w_router.shape[-1]
    att_dim = attn_norm_g.shape[-1]
    n_tok = b * t
    tm_ffn = 1024

    mod = _adaln_mod(c, w_mod, b_mod)
    mod4 = mod.reshape(depth, b, 1, 6 * d)

    gate0 = 2 * conv_ch + 4 * ml_dim
    gate1 = gate0 + 2 * ml_heads
    att_col_scale = jnp.concatenate([jnp.full((att_dim,), HEAD_DIM ** -0.5, f32), jnp.ones((2 * att_dim,), f32)])
    w_in_a = w_in[:, :, :gate0]
    w_in_gate = w_in[:, :, gate0:gate1]
    w_in_att = w_in[:, :, gate1:] * att_col_scale[None, None, :]

    h = _prep(x, g_mix[0], mod4, 0, 0)
    out = None
    for l in range(depth):
        z, zg = _in_projection(h.reshape(n_tok, d), w_in_a, w_in_att, w_in_gate, l)
        z = z.reshape(b, t, -1)
        zg = zg.reshape(b, t, -1)
        zg_t = jnp.swapaxes(zg, 1, 2)

        y_conv = _conv_module(z, conv_w[l], conv_b[l], conv_ln_g[l], conv_ln_b[l])
        y_ml = _mlstm(z, zg_t, (2 * conv_ch) // ml_dim, ml_conv_w[l], ml_conv_b[l], ml_b_i[l], ml_b_f[l], ml_norm_g[l])
        y_att = _moba(z, gate0 // HEAD_DIM, attn_norm_g[l])

        j = l // 2
        w_o = w_out[l].astype(bf16)
        if l % 2 == 0:
            x, h = _out_projection(y_conv, y_ml, y_att, w_o, x, mod4, l, g_ffn[l])
            y = _dense_swiglu(h.reshape(n_tok, d), ffn_w_gate, ffn_w_up, ffn_w_down, j, tm=tm_ffn)
            ys, y_probs = [y.reshape(b, t, d)], None
        else:
            x, h32, idx, probs = _out_projection(y_conv, y_ml, y_att, w_o, x, mod4, l, g_ffn[l],
                                                 router=(moe_w_router[j], moe_b_router[j]))
            slot, row_token, tile_expert, tile_quarters = _routing_tables(
                jnp.swapaxes(idx, 1, 2).reshape(n_tok, TOP_K), n_experts, tm_ffn)
            ff = moe_w_gate.shape[-1]
            y_sorted = _moe_swiglu(
                h32.reshape(n_tok, d), row_token, tile_expert + j * n_experts, tile_quarters,
                moe_w_gate.reshape(-1, d, ff), moe_w_up.reshape(-1, d, ff), moe_w_down.reshape(-1, ff, d), tm=tm_ffn)
            ys = [y_sorted.at[slot[:, k]].get(mode="promise_in_bounds").reshape(b, t, d) for k in range(TOP_K)]
            y_probs = jnp.swapaxes(probs, 1, 2)

        if l + 1 < depth:
            x, h = _combine(x, mod4, l, ys, y_probs, g_mix[l + 1], last=False)
        else:
            out = _combine(x, mod4, l, ys, y_probs, g_final, last=True)
    return out
```

```python
import functools

import jax
import jax.numpy as jnp
from jax import lax
from jax.experimental import pallas as pl
from jax.experimental.pallas import tpu as pltpu

f32 = jnp.float32
bf16 = jnp.bfloat16
HIGHEST = lax.Precision.HIGHEST

CONV_GROUPS = 4
HEAD_DIM = 128
ML_CHUNK = 128
MOBA_BLOCK = 256
MOBA_TOPK = 3
TOP_K = 2
EPS = 1e-6
NEG = -1e30

VMEM_LIMIT_BYTES = 56 * 1024 * 1024
LANE = 128


def _cparams(*sem):
    return pltpu.CompilerParams(dimension_semantics=sem, vmem_limit_bytes=VMEM_LIMIT_BYTES)


def _sigmoid(v):
    return 1.0 / (1.0 + jnp.exp(-v))


def _silu(v):
    return v * _sigmoid(v)


def _dot_nt(a, b, **kw):
    return lax.dot_general(a, b, (((1,), (1,)), ((), ())), preferred_element_type=f32, **kw)


def _dot_tn(a, b, **kw):
    return lax.dot_general(a, b, (((0,), (0,)), ((), ())), preferred_element_type=f32, **kw)


def _mod_kernel(c_ref, w_ref, b_ref, o_ref):
    cond = _silu(c_ref[...]).astype(bf16)
    o_ref[...] = jnp.dot(cond, w_ref[...].astype(bf16), preferred_element_type=f32) + b_ref[...]


def _adaln_mod(c, w_mod, b_mod, tn=1024):
    depth, d, n6 = w_mod.shape
    b = c.shape[0]
    return pl.pallas_call(
        _mod_kernel,
        grid=(depth, n6 // tn),
        in_specs=[
            pl.BlockSpec((b, d), lambda l, n: (0, 0)),
            pl.BlockSpec((None, d, tn), lambda l, n: (l, 0, n)),
            pl.BlockSpec((None, 1, tn), lambda l, n: (l, 0, n)),
        ],
        out_specs=pl.BlockSpec((None, b, tn), lambda l, n: (l, 0, n)),
        out_shape=jax.ShapeDtypeStruct((depth, b, n6), f32),
        compiler_params=_cparams("parallel", "parallel"),
        name="adaln_mod",
    )(c, w_mod, b_mod.reshape(depth, 1, n6))


def _norm_modulate(x, g, sh, sc):
    ms = jnp.mean(x * x, axis=-1, keepdims=True)
    h = x * lax.rsqrt(ms + EPS) * g
    return h * (1.0 + sc) + sh


def _prep_kernel(x_ref, g_ref, sh_ref, sc_ref, h_ref):
    h_ref[...] = _norm_modulate(x_ref[...], g_ref[...], sh_ref[...], sc_ref[...]).astype(h_ref.dtype)


def _prep(x, g, mod4, layer, chunk, tm=512):
    b, t, d = x.shape
    return pl.pallas_call(
        _prep_kernel,
        grid=(b, t // tm),
        in_specs=[
            pl.BlockSpec((None, tm, d), lambda bi, i: (bi, i, 0)),
            pl.BlockSpec((1, d), lambda bi, i: (0, 0)),
            pl.BlockSpec((None, None, 1, d), lambda bi, i: (layer, bi, 0, chunk)),
            pl.BlockSpec((None, None, 1, d), lambda bi, i: (layer, bi, 0, chunk + 1)),
        ],
        out_specs=pl.BlockSpec((None, tm, d), lambda bi, i: (bi, i, 0)),
        out_shape=jax.ShapeDtypeStruct((b, t, d), bf16),
        compiler_params=_cparams("parallel", "parallel"),
        name="prep",
    )(x, g.reshape(1, d), mod4, mod4)


def _inproj_kernel(h_ref, wa_ref, wb_ref, wg_ref, z_ref, zg_ref, *, n_a):
    j = pl.program_id(1)

    @pl.when(j == 0)
    def _():
        zg_ref[...] = jnp.dot(h_ref[...], wg_ref[...].astype(bf16), preferred_element_type=f32)

    @pl.when(j < n_a)
    def _():
        z_ref[...] = jnp.dot(h_ref[...], wa_ref[...].astype(bf16), preferred_element_type=f32).astype(bf16)

    @pl.when(j >= n_a)
    def _():
        z_ref[...] = jnp.dot(h_ref[...], wb_ref[...].astype(bf16), preferred_element_type=f32).astype(bf16)


def _in_projection(h, w_a, w_b, w_gate, layer, tm=2048, tn=512):
    n, d = h.shape
    n_a = w_a.shape[-1] // tn
    n_b = w_b.shape[-1] // tn
    ng = w_gate.shape[-1]
    assert n_a * tn == w_a.shape[-1] and n_b * tn == w_b.shape[-1]
    return pl.pallas_call(
        functools.partial(_inproj_kernel, n_a=n_a),
        grid=(n // tm, n_a + n_b),
        in_specs=[
            pl.BlockSpec((tm, d), lambda i, j: (i, 0)),
            pl.BlockSpec((None, d, tn), lambda i, j: (layer, 0, jnp.minimum(j, n_a - 1))),
            pl.BlockSpec((None, d, tn), lambda i, j: (layer, 0, jnp.maximum(j - n_a, 0))),
            pl.BlockSpec((None, d, ng), lambda i, j: (layer, 0, 0)),
        ],
        out_specs=[
            pl.BlockSpec((tm, tn), lambda i, j: (i, j)),
            pl.BlockSpec((tm, ng), lambda i, j: (i, 0)),
        ],
        out_shape=[
            jax.ShapeDtypeStruct((n, (n_a + n_b) * tn), bf16),
            jax.ShapeDtypeStruct((n, ng), f32),
        ],
        compiler_params=_cparams("parallel", "arbitrary"),
        name="in_projection",
    )(h, w_a, w_b, w_gate)


CONV_PAD = 32
ROWS = 128


def _conv_kernel(ag_ref, w_ref, b_ref, lg_ref, lb_ref, o_ref, xp_ref):
    t, c = o_ref.shape
    taps = w_ref.shape[0]
    gc = c // CONV_GROUPS
    for cg in range(CONV_GROUPS):
        cs = slice(cg * gc, (cg + 1) * gc)
        gs = slice(c + cg * gc, c + (cg + 1) * gc)
        xp_ref[cg, 0:CONV_PAD, :] = jnp.zeros((CONV_PAD, gc), f32)

        def glu(i, _, cg=cg, cs=cs, gs=gs):
            r0 = pl.multiple_of(i * ROWS, ROWS)
            a = ag_ref[pl.ds(r0, ROWS), cs].astype(f32)
            g = ag_ref[pl.ds(r0, ROWS), gs].astype(f32)
            xp_ref[cg, pl.ds(CONV_PAD + r0, ROWS), :] = a * _sigmoid(g)
            return 0

        lax.fori_loop(0, t // ROWS, glu, 0)

        def body(i, _, cg=cg, cs=cs):
            r0 = pl.multiple_of(i * ROWS, ROWS)
            acc = jnp.zeros((ROWS, gc), f32) + b_ref[:, cs]
            for k in range(taps):
                acc = acc + xp_ref[cg, pl.ds(r0 + (CONV_PAD - (taps - 1) + k), ROWS), :] * w_ref[k:k + 1, cs]
            mu = jnp.mean(acc, axis=-1, keepdims=True)
            dv = acc - mu
            var = jnp.mean(dv * dv, axis=-1, keepdims=True)
            y = dv * lax.rsqrt(var + EPS) * lg_ref[:, cs] + lb_ref[:, cs]
            o_ref[pl.ds(r0, ROWS), cs] = _silu(y).astype(o_ref.dtype)
            return 0

        lax.fori_loop(0, t // ROWS, body, 0, unroll=2)


def _conv_module(z, conv_w, conv_b, ln_g, ln_b):
    b, t, _ = z.shape
    taps, c = conv_w.shape
    assert taps - 1 <= CONV_PAD and c // CONV_GROUPS == LANE
    return pl.pallas_call(
        _conv_kernel,
        grid=(b,),
        in_specs=[
            pl.BlockSpec((None, t, 2 * c), lambda bi: (bi, 0, 0)),
            pl.BlockSpec((taps, c), lambda bi: (0, 0)),
            pl.BlockSpec((1, c), lambda bi: (0, 0)),
            pl.BlockSpec((1, c), lambda bi: (0, 0)),
            pl.BlockSpec((1, c), lambda bi: (0, 0)),
        ],
        out_specs=pl.BlockSpec((None, t, c), lambda bi: (bi, 0, 0)),
        out_shape=jax.ShapeDtypeStruct((b, t, c), bf16),
        scratch_shapes=[pltpu.VMEM((CONV_GROUPS, t + CONV_PAD, c // CONV_GROUPS), f32)],
        compiler_params=_cparams("parallel"),
        name="conv_module",
    )(z, conv_w, conv_b.reshape(1, c), ln_g.reshape(1, c), ln_b.reshape(1, c))


ML_PAD = 8


def _mlstm_kernel(q_ref, k_ref, v_ref, o_ref, gr_ref, cw_ref, cb_ref, bic_ref, bfc_ref,
                  ng_ref, y_ref, qp_ref, kp_ref, qs_ref, ks_ref, va_ref, cta_ref, m_ref):
    t, dm = q_ref.shape
    nh = dm // HEAD_DIM
    L = ML_CHUNK
    nc = t // L
    taps = cw_ref.shape[0]
    kscale = HEAD_DIM ** -0.5

    for h in range(nh):
        hs = slice(h * HEAD_DIM, (h + 1) * HEAD_DIM)
        ks_cols = slice(dm + h * HEAD_DIM, dm + (h + 1) * HEAD_DIM)
        qp_ref[h, 0:ML_PAD, :] = jnp.zeros((ML_PAD, HEAD_DIM), f32)
        kp_ref[h, 0:ML_PAD, :] = jnp.zeros((ML_PAD, HEAD_DIM), f32)

        def stage(i, _, h=h, hs=hs):
            r0 = pl.multiple_of(i * ROWS, ROWS)
            qp_ref[h, pl.ds(ML_PAD + r0, ROWS), :] = q_ref[pl.ds(r0, ROWS), hs].astype(f32)
            kp_ref[h, pl.ds(ML_PAD + r0, ROWS), :] = k_ref[pl.ds(r0, ROWS), hs].astype(f32)
            return 0

        lax.fori_loop(0, t // ROWS, stage, 0)

        def sconv(i, _, h=h, hs=hs, ks_cols=ks_cols):
            r0 = pl.multiple_of(i * ROWS, ROWS)
            aq = jnp.zeros((ROWS, HEAD_DIM), f32) + cb_ref[:, hs]
            ak = jnp.zeros((ROWS, HEAD_DIM), f32) + cb_ref[:, ks_cols]
            for j in range(taps):
                off = ML_PAD - (taps - 1) + j
                aq = aq + qp_ref[h, pl.ds(r0 + off, ROWS), :] * cw_ref[j:j + 1, hs]
                ak = ak + kp_ref[h, pl.ds(r0 + off, ROWS), :] * cw_ref[j:j + 1, ks_cols]
            qs_ref[pl.ds(r0, ROWS), hs] = _silu(aq).astype(bf16)
            ks_ref[pl.ds(r0, ROWS), hs] = (_silu(ak) * kscale).astype(bf16)
            return 0

        lax.fori_loop(0, t // ROWS, sconv, 0)

    for h in range(nh):
        hs = slice(h * HEAD_DIM, (h + 1) * HEAD_DIM)
        va_ref[:, 2 * h * HEAD_DIM:(2 * h + 1) * HEAD_DIM] = v_ref[:, hs]
        va_ref[:, (2 * h + 1) * HEAD_DIM:(2 * h + 2) * HEAD_DIM] = jnp.ones((t, HEAD_DIM), bf16)

    ri = lax.broadcasted_iota(jnp.int32, (L, L), 0)
    ci = lax.broadcasted_iota(jnp.int32, (L, L), 1)
    causal = ri >= ci
    lower_f = causal.astype(f32)
    eye_f = (ri == ci).astype(f32)
    eye_b = eye_f.astype(bf16)
    upper_b = (ri <= ci).astype(bf16)
    ones_b = jnp.ones((L, HEAD_DIM), bf16)
    lane = lax.broadcasted_iota(jnp.int32, (nh, L), 1)

    def log_sigmoid(v):
        return jnp.minimum(v, 0.0) - jnp.log(1.0 + jnp.exp(-jnp.abs(v)))

    def split(a, pieces):
        out = []
        for _ in range(pieces - 1):
            p = a.astype(bf16)
            out.append(p)
            a = a - p.astype(f32)
        out.append(a.astype(bf16))
        return out

    def rows_to_columns(weighted):
        return sum(jnp.dot(p, ones_b, preferred_element_type=f32) for p in split(weighted, 2))

    cta_ref[...] = jnp.zeros_like(cta_ref)
    m_ref[...] = jnp.zeros_like(m_ref)

    def chunk(i, _):
        r0 = pl.multiple_of(i * L, L)
        li = gr_ref[0:nh, pl.ds(r0, L)] + bic_ref[...]
        lf = log_sigmoid(gr_ref[nh:2 * nh, pl.ds(r0, L)] + bfc_ref[...])
        b = sum(jnp.dot(p, upper_b, preferred_element_type=f32) for p in split(lf, 3))
        g = li - b
        cmax = g
        for sh in (1, 2, 4, 8, 16, 32, 64):
            cmax = jnp.maximum(cmax, jnp.where(lane >= sh, pltpu.roll(cmax, sh, axis=1), NEG))
        m_prev = m_ref[...]
        big_m = jnp.maximum(cmax, m_prev)
        m_last = big_m[:, L - 1:L]
        b_last = b[:, L - 1:L]
        wk = jnp.exp(g - m_last)
        decay = jnp.exp(m_prev[:, 0:1] - m_last)
        m_ref[...] = jnp.broadcast_to(b_last + m_last, (nh, L))

        heads = range(nh)
        hsl = [slice(h * HEAD_DIM, (h + 1) * HEAD_DIM) for h in heads]
        q = [qs_ref[pl.ds(r0, L), hsl[h]] for h in heads]
        k = [ks_ref[pl.ds(r0, L), hsl[h]] for h in heads]
        va = [va_ref[pl.ds(r0, L), 2 * h * HEAD_DIM:(2 * h + 2) * HEAD_DIM] for h in heads]
        cta = [cta_ref[h] for h in heads]
        m_col = [rows_to_columns(eye_f * big_m[h:h + 1, :]) for h in heads]
        b_col = [rows_to_columns(lower_f * lf[h:h + 1, :]) for h in heads]
        qk = [_dot_nt(q[h], k[h]) for h in heads]
        k_t = [_dot_nt(eye_b, k[h]) for h in heads]
        inter = [jnp.dot(q[h], cta[h].astype(bf16), preferred_element_type=f32) for h in heads]
        s = [(qk[h] * jnp.exp(jnp.where(causal, g[h:h + 1, :] - m_col[h], NEG))).astype(bf16) for h in heads]
        kw_t = [(k_t[h] * wk[h:h + 1, :]).astype(bf16) for h in heads]
        w_inter = [jnp.exp(m_prev[h:h + 1, 0:1] - m_col[h]) for h in heads]
        both = [jnp.dot(s[h], va[h], preferred_element_type=f32)
                + jnp.concatenate([w_inter[h], w_inter[h]], axis=1) * inter[h] for h in heads]
        new_cta = [decay[h:h + 1, :] * cta[h] + jnp.dot(kw_t[h], va[h], preferred_element_type=f32) for h in heads]
        hout = [both[h][:, 0:HEAD_DIM] / jnp.maximum(jnp.abs(both[h][:, HEAD_DIM:2 * HEAD_DIM]),
                                                     jnp.exp(-(b_col[h] + m_col[h]))) for h in heads]
        ssq = [sum(jnp.dot(pc, ones_b, preferred_element_type=f32) for pc in split(hout[h] * hout[h], 2)) for h in heads]
        for h in heads:
            hn = hout[h] * lax.rsqrt(ssq[h] * (1.0 / HEAD_DIM) + EPS) * ng_ref[:, hsl[h]]
            og = _sigmoid(o_ref[pl.ds(r0, L), hsl[h]].astype(f32))
            y_ref[pl.ds(r0, L), hsl[h]] = (og * hn).astype(y_ref.dtype)
            cta_ref[h] = new_cta[h]
        return 0

    lax.fori_loop(0, nc, chunk, 0, unroll=2)


def _mlstm(z, zg_t, col0, ml_conv_w, ml_conv_b, b_i, b_f, norm_g):
    b, t, _ = z.shape
    dm = norm_g.shape[0]
    nh = dm // HEAD_DIM
    taps = ml_conv_w.shape[0]
    assert taps - 1 <= ML_PAD

    def zspec(j):
        return pl.BlockSpec((None, t, dm), lambda bi: (bi, 0, col0 + j))

    return pl.pallas_call(
        _mlstm_kernel,
        grid=(b,),
        in_specs=[
            zspec(0), zspec(1), zspec(2), zspec(3),
            pl.BlockSpec((None, 2 * nh, t), lambda bi: (bi, 0, 0)),
            pl.BlockSpec((taps, 2 * dm), lambda bi: (0, 0)),
            pl.BlockSpec((1, 2 * dm), lambda bi: (0, 0)),
            pl.BlockSpec((nh, 1), lambda bi: (0, 0)),
            pl.BlockSpec((nh, 1), lambda bi: (0, 0)),
            pl.BlockSpec((1, dm), lambda bi: (0, 0)),
        ],
        out_specs=pl.BlockSpec((None, t, dm), lambda bi: (bi, 0, 0)),
        out_shape=jax.ShapeDtypeStruct((b, t, dm), bf16),
        scratch_shapes=[
            pltpu.VMEM((nh, t + ML_PAD, HEAD_DIM), f32),
            pltpu.VMEM((nh, t + ML_PAD, HEAD_DIM), f32),
            pltpu.VMEM((t, dm), bf16),
            pltpu.VMEM((t, dm), bf16),
            pltpu.VMEM((t, 2 * dm), bf16),
            pltpu.VMEM((nh, HEAD_DIM, 2 * HEAD_DIM), f32),
            pltpu.VMEM((nh, ML_CHUNK), f32),
        ],
        compiler_params=_cparams("parallel"),
        name="mlstm",
    )(z, z, z, z, zg_t, ml_conv_w, ml_conv_b.reshape(1, 2 * dm), b_i.reshape(nh, 1), b_f.reshape(nh, 1),
      norm_g.reshape(1, dm))


MASK_BIG = 1e30
BIAS_COLS = 128


def _moba_kernel(q_ref, k_ref, v_ref, kx_ref, sl_ref, ng_ref, y_ref, km_ref, s_ref, kc_ref, vc_ref):
    t, dh = q_ref.shape
    blk = MOBA_BLOCK
    nb = t // blk

    kc_ref[:, 0:dh] = k_ref[...]
    kc_ref[:, dh:dh + BIAS_COLS] = kx_ref[...]
    vc_ref[:, 0:dh] = v_ref[...]
    vc_ref[:, dh:2 * dh] = jnp.ones((t, dh), bf16)

    for n in range(nb):
        km_ref[n:n + 1, :] = jnp.mean(k_ref[n * blk:(n + 1) * blk, :].astype(f32), axis=0, keepdims=True)
    gate_t = _dot_nt(km_ref[...], q_ref[...].astype(f32), precision=HIGHEST)

    blk_id = lax.broadcasted_iota(jnp.int32, (nb, blk), 0)
    ri = lax.broadcasted_iota(jnp.int32, (blk, blk), 0)
    ci = lax.broadcasted_iota(jnp.int32, (blk, blk), 1)
    causal = ri >= ci
    eye = (ri == ci).astype(bf16)
    lane = lax.broadcasted_iota(jnp.int32, (blk, BIAS_COLS), 1)
    rowf = lax.broadcasted_iota(jnp.int32, (blk, BIAS_COLS), 0).astype(f32)
    slope = sl_ref[:, 0:1]

    def logits_pass(j):
        qj = q_ref[j * blk:(j + 1) * blk, :]
        masked_t = jnp.zeros((nb, blk), f32)
        if j > 0:
            g = jnp.where(blk_id < j, gate_t[:, j * blk:(j + 1) * blk], NEG)
            for n in range(j):
                gn = g[n:n + 1, :]
                ahead = (g > gn) | ((g == gn) & (blk_id < n))
                rank = jnp.sum(jnp.where(ahead, 1.0, 0.0), axis=0, keepdims=True)
                masked_t = jnp.where((blk_id == n) & (rank >= float(MOBA_TOPK)), -1.0, masked_t)
        masked_pad = jnp.concatenate([masked_t, jnp.zeros((BIAS_COLS - nb, blk), f32)], axis=0).astype(bf16)
        qx = _dot_nt(eye, masked_pad)
        qx = jnp.where(lane == nb, -slope * float(blk * j), qx)
        qx = jnp.where(lane == nb + 1, -slope * rowf, qx)
        qx = jnp.where((lane == nb + 2) | (lane == nb + 3), 1.0, qx).astype(bf16)
        qc = jnp.concatenate([qj, qx], axis=1)
        mx = None
        for n in range(j + 1):
            ks = slice(n * blk, (n + 1) * blk)
            lg = _dot_nt(qc, kc_ref[ks, :])
            if n == j:
                lg = jnp.where(causal, lg, NEG)
            s_ref[j % 2, :, ks] = lg
            half = jnp.maximum(lg[:, 0:LANE], lg[:, LANE:2 * LANE])
            mx = half if mx is None else jnp.maximum(mx, half)
        return jnp.max(mx, axis=-1, keepdims=True)

    def value_pass(j, m):
        acc = jnp.zeros((blk, 2 * dh), f32)
        for n in range(j + 1):
            ks = slice(n * blk, (n + 1) * blk)
            p = jnp.exp((s_ref[j % 2, :, ks] - m).astype(bf16))
            acc = acc + jnp.dot(p, vc_ref[ks, :], preferred_element_type=f32)
        o = acc[:, 0:dh] / acc[:, dh:2 * dh]
        o = o * lax.rsqrt(jnp.mean(o * o, axis=-1, keepdims=True) + EPS) * ng_ref[...]
        y_ref[j * blk:(j + 1) * blk, :] = o.astype(y_ref.dtype)

    m_prev = logits_pass(0)
    for j in range(1, nb):
        m_cur = logits_pass(j)
        value_pass(j - 1, m_prev)
        m_prev = m_cur
    value_pass(nb - 1, m_prev)


def _moba(z, col0, norm_g):
    b, t, _ = z.shape
    dh = HEAD_DIM
    nh = norm_g.shape[0] // dh
    blk = MOBA_BLOCK
    assert t % blk == 0
    nb = t // blk
    assert nb + 4 <= BIAS_COLS
    slopes = 2.0 ** (-(8.0 / nh) * jnp.arange(1, nh + 1, dtype=f32))
    pos = jnp.arange(t, dtype=jnp.int32)
    col = jnp.arange(BIAS_COLS, dtype=jnp.int32)[None, :]
    kblk = (pos // blk)[:, None]
    base = jnp.where(col == kblk, MASK_BIG, 0.0) + jnp.where((col == nb) | (col == nb + 1), 1.0, 0.0)
    per_head = (jnp.where(col == nb + 2, (kblk * blk).astype(f32), 0.0)
                + jnp.where(col == nb + 3, (pos % blk).astype(f32)[:, None], 0.0))
    kx = (base[None] + slopes[:, None, None] * per_head[None]).astype(bf16)
    slope_rows = jnp.broadcast_to(slopes[:, None, None], (nh, 1, LANE))

    def zspec(j):
        return pl.BlockSpec((None, t, dh), lambda bi, h: (bi, 0, col0 + j * nh + h))

    return pl.pallas_call(
        _moba_kernel,
        grid=(b, nh),
        in_specs=[
            zspec(0), zspec(1), zspec(2),
            pl.BlockSpec((None, t, BIAS_COLS), lambda bi, h: (h, 0, 0)),
            pl.BlockSpec((None, 1, LANE), lambda bi, h: (h, 0, 0)),
            pl.BlockSpec((None, 1, dh), lambda bi, h: (h, 0, 0)),
        ],
        out_specs=pl.BlockSpec((None, t, dh), lambda bi, h: (bi, 0, h)),
        out_shape=jax.ShapeDtypeStruct((b, t, nh * dh), bf16),
        scratch_shapes=[pltpu.VMEM((nb, dh), f32), pltpu.VMEM((2, blk, t), f32), pltpu.VMEM((t, dh + BIAS_COLS), bf16),
                        pltpu.VMEM((t, 2 * dh), bf16)],
        compiler_params=_cparams("parallel", "parallel"),
        name="moba_attention",
    )(z, z, z, kx, slope_rows, norm_g.reshape(nh, 1, dh))


def _router_logits_t(w_t32, h32, hb):
    h_lo = (h32 - hb.astype(f32)).astype(bf16)
    w_hi = w_t32.astype(bf16)
    w_lo = (w_t32 - w_hi.astype(f32)).astype(bf16)
    return _dot_nt(w_hi, hb) + _dot_nt(w_hi, h_lo) + _dot_nt(w_lo, hb)


def _top2_t(logits):
    ne = logits.shape[0]
    eid = lax.broadcasted_iota(jnp.int32, logits.shape, 0)
    m1 = jnp.max(logits, axis=0, keepdims=True)
    i1 = jnp.min(jnp.where(logits == m1, eid, ne), axis=0, keepdims=True)
    rest = jnp.where(eid == i1, -jnp.inf, logits)
    m2 = jnp.max(rest, axis=0, keepdims=True)
    i2 = jnp.min(jnp.where(rest == m2, eid, ne), axis=0, keepdims=True)
    e2 = jnp.exp(m2 - m1)
    p1 = 1.0 / (1.0 + e2)
    return i1, i2, p1, e2 * p1


OUTPROJ_SPLIT = 2


def _outproj_kernel(yc_ref, ym_ref, ya_ref, w_ref, x_ref, gt_ref, g_ref, sh_ref, sc_ref, *rest, routed):
    if routed:
        wr_ref, br_ref, xo_ref, h_ref, idx_ref, p_ref = rest
    else:
        xo_ref, h_ref = rest
    tm = x_ref.shape[0]
    sub = tm // OUTPROJ_SPLIT
    blocks = [slice(r * sub, (r + 1) * sub) for r in range(OUTPROJ_SPLIT)]
    accs = [jnp.dot(jnp.concatenate([yc_ref[rs, :], ym_ref[rs, :], ya_ref[rs, :]], axis=1), w_ref[...],
                    preferred_element_type=f32) for rs in blocks]
    for rs, acc in zip(blocks, accs):
        xn = x_ref[rs, :] + gt_ref[...] * acc
        xo_ref[rs, :] = xn
        h = _norm_modulate(xn, g_ref[...], sh_ref[...], sc_ref[...])
        hb = h.astype(bf16)
        h_ref[rs, :] = h if routed else hb
        if routed:
            i1, i2, p1, p2 = _top2_t(_router_logits_t(wr_ref[...], h, hb) + br_ref[...])
            first = lax.broadcasted_iota(jnp.int32, (TOP_K, sub), 0) == 0
            idx_ref[:, rs] = jnp.where(first, i1, i2)
            p_ref[:, rs] = jnp.where(first, p1, p2)


def _out_projection(y_conv, y_ml, y_att, w_out, x, mod4, layer, g_ffn, router=None, tm=512):
    b, t, d = x.shape
    c1, c2, c3 = y_conv.shape[-1], y_ml.shape[-1], y_att.shape[-1]
    assert c1 + c2 + c3 == w_out.shape[0]
    row = lambda c: pl.BlockSpec((None, tm, c), lambda bi, i: (bi, i, 0))
    modspec = lambda chunk: pl.BlockSpec((None, None, 1, d), lambda bi, i: (layer, bi, 0, chunk))
    in_specs = [
        row(c1), row(c2), row(c3),
        pl.BlockSpec(w_out.shape, lambda bi, i: (0, 0)),
        row(d), modspec(2),
        pl.BlockSpec((1, d), lambda bi, i: (0, 0)), modspec(3), modspec(4),
    ]
    args = [y_conv, y_ml, y_att, w_out, x, mod4, g_ffn.reshape(1, d), mod4, mod4]
    out_specs = [row(d), row(d)]
    out_shape = [jax.ShapeDtypeStruct((b, t, d), f32), jax.ShapeDtypeStruct((b, t, d), bf16 if router is None else f32)]
    if router is not None:
        w_r, b_r = router
        ne = w_r.shape[1]
        in_specs += [pl.BlockSpec((ne, d), lambda bi, i: (0, 0)), pl.BlockSpec((ne, 1), lambda bi, i: (0, 0))]
        args += [w_r.T, b_r.reshape(ne, 1)]
        kspec = pl.BlockSpec((None, TOP_K, tm), lambda bi, i: (bi, 0, i))
        out_specs += [kspec, kspec]
        out_shape += [jax.ShapeDtypeStruct((b, TOP_K, t), jnp.int32), jax.ShapeDtypeStruct((b, TOP_K, t), f32)]
    return pl.pallas_call(
        functools.partial(_outproj_kernel, routed=router is not None),
        grid=(b, t // tm),
        in_specs=in_specs,
        out_specs=out_specs,
        out_shape=out_shape,
        compiler_params=_cparams("parallel", "parallel"),
        name="out_projection",
    )(*args)


def _swiglu_rows(hb, wg_ref, wu_ref, wd_ref):
    g = jnp.dot(hb, wg_ref[...].astype(bf16), preferred_element_type=f32)
    u = jnp.dot(hb, wu_ref[...].astype(bf16), preferred_element_type=f32)
    a = (_silu(g) * u).astype(bf16)
    return jnp.dot(a, wd_ref[...].astype(bf16), preferred_element_type=f32)


def _dense_swiglu_kernel(h_ref, wg_ref, wu_ref, wd_ref, o_ref, acc_ref):
    f = pl.program_id(1)

    @pl.when(f == 0)
    def _():
        acc_ref[...] = jnp.zeros_like(acc_ref)

    acc_ref[...] += _swiglu_rows(h_ref[...], wg_ref, wu_ref, wd_ref)

    @pl.when(f == pl.num_programs(1) - 1)
    def _():
        o_ref[...] = acc_ref[...].astype(o_ref.dtype)


def _dense_swiglu(h, wg, wu, wd, layer, tm=1024, tf=512):
    n, d = h.shape
    ff = wg.shape[-1]
    return pl.pallas_call(
        _dense_swiglu_kernel,
        grid=(n // tm, ff // tf),
        in_specs=[
            pl.BlockSpec((tm, d), lambda i, f: (i, 0)),
            pl.BlockSpec((None, d, tf), lambda i, f: (layer, 0, f)),
            pl.BlockSpec((None, d, tf), lambda i, f: (layer, 0, f)),
            pl.BlockSpec((None, tf, d), lambda i, f: (layer, f, 0)),
        ],
        out_specs=pl.BlockSpec((tm, d), lambda i, f: (i, 0)),
        out_shape=jax.ShapeDtypeStruct((n, d), bf16),
        scratch_shapes=[pltpu.VMEM((tm, d), f32)],
        compiler_params=_cparams("parallel", "arbitrary"),
        name="dense_swiglu",
    )(h, wg, wu, wd)


QUARTER = 256
GATHER_UNROLL = 8


def _moe_swiglu_kernel(te_ref, nq_ref, tok_ref, tok_next_ref, h_hbm, wg_ref, wu_ref, wd_ref, o_ref,
                       hrows_ref, hb_ref, acc_ref, sem):
    i = pl.program_id(0)
    f = pl.program_id(1)
    nt = pl.num_programs(0)
    nq = nq_ref[i]
    tm = acc_ref.shape[0]

    def row_copy(src_row, dst_row, rows=1):
        return pltpu.make_async_copy(h_hbm.at[pl.ds(src_row, rows)], hrows_ref.at[pl.ds(dst_row, rows)], sem)

    def start_gather(tokens_ref, quarters):
        def issue(g, _):
            for u in range(GATHER_UNROLL):
                r = g * GATHER_UNROLL + u
                row_copy(tokens_ref[0, r], r).start()
            return 0

        lax.fori_loop(0, quarters * (QUARTER // GATHER_UNROLL), issue, 0)

    @pl.when(f == 0)
    def _():
        acc_ref[...] = jnp.zeros_like(acc_ref)

        @pl.when((i == 0) & (nq > 0))
        def _():
            start_gather(tok_ref, nq)

        @pl.when(nq > 0)
        def _():
            row_copy(0, 0, nq * QUARTER).wait()
            for q in range(tm // QUARTER):
                @pl.when(q < nq)
                def _(q=q):
                    qs = slice(q * QUARTER, (q + 1) * QUARTER)
                    hb_ref[qs, :] = hrows_ref[qs, :].astype(bf16)

        nq_next = nq_ref[jnp.minimum(i + 1, nt - 1)]

        @pl.when((i + 1 < nt) & (nq_next > 0))
        def _():
            start_gather(tok_next_ref, nq_next)

    for q in range(1, tm // QUARTER + 1):
        @pl.when(nq == q)
        def _(q=q):
            rows = q * QUARTER
            acc_ref[0:rows, :] += _swiglu_rows(hb_ref[0:rows, :], wg_ref, wu_ref, wd_ref)

    @pl.when(f == pl.num_programs(1) - 1)
    def _():
        o_ref[...] = acc_ref[...].astype(o_ref.dtype)


def _moe_swiglu(h32, row_token, tile_expert, tile_quarters, wg, wu, wd, tm=1024, tf=256):
    n, d = h32.shape
    nt = row_token.shape[0]
    ff = wg.shape[-1]
    nf = ff // tf
    assert tm % QUARTER == 0

    def wmap(i, f, te, nq):
        return (te[i], 0, jnp.where(nq[i] > 0, f, nf - 1))

    def wdmap(i, f, te, nq):
        return (te[i], jnp.where(nq[i] > 0, f, nf - 1), 0)

    grid_spec = pltpu.PrefetchScalarGridSpec(
        num_scalar_prefetch=2,
        grid=(nt, nf),
        in_specs=[
            pl.BlockSpec((None, 1, tm), lambda i, f, te, nq: (i, 0, 0), memory_space=pltpu.SMEM),
            pl.BlockSpec((None, 1, tm), lambda i, f, te, nq: (jnp.minimum(i + 1, nt - 1), 0, 0),
                         memory_space=pltpu.SMEM),
            pl.BlockSpec(memory_space=pl.ANY),
            pl.BlockSpec((None, d, tf), wmap),
            pl.BlockSpec((None, d, tf), wmap),
            pl.BlockSpec((None, tf, d), wdmap),
        ],
        out_specs=pl.BlockSpec((tm, d), lambda i, f, te, nq: (i, 0)),
        scratch_shapes=[
            pltpu.VMEM((tm, d), f32),
            pltpu.VMEM((tm, d), bf16),
            pltpu.VMEM((tm, d), f32),
            pltpu.SemaphoreType.DMA,
        ],
    )
    return pl.pallas_call(
        _moe_swiglu_kernel,
        grid_spec=grid_spec,
        out_shape=jax.ShapeDtypeStruct((nt * tm, d), bf16),
        compiler_params=_cparams("arbitrary", "arbitrary"),
        name="moe_swiglu",
    )(tile_expert, tile_quarters, row_token, row_token, h32, wg, wu, wd)


def _combine_kernel(*refs, n_y, weighted, last):
    x_ref, gt_ref = refs[0], refs[1]
    ys = [r[...].astype(f32) for r in refs[2:2 + n_y]]
    rest = refs[2 + n_y:]
    if weighted:
        p = rest[0][...]
        rest = rest[1:]
        ys = [p[:, k:k + 1] * y for k, y in enumerate(ys)]
    y = ys[0]
    for yk in ys[1:]:
        y = y + yk
    xn = x_ref[...] + gt_ref[...] * y
    if last:
        g_ref, o_ref = rest
        o_ref[...] = xn * lax.rsqrt(jnp.mean(xn * xn, axis=-1, keepdims=True) + EPS) * g_ref[...]
    else:
        g_ref, sh_ref, sc_ref, xo_ref, h_ref = rest
        xo_ref[...] = xn
        h_ref[...] = _norm_modulate(xn, g_ref[...], sh_ref[...], sc_ref[...]).astype(h_ref.dtype)


def _combine(x, mod4, layer, ys, probs, g_next, last, tm=512):
    b, t, d = x.shape
    spec = pl.BlockSpec((None, tm, d), lambda bi, i: (bi, i, 0))
    modspec = lambda lyr, chunk: pl.BlockSpec((None, None, 1, d), lambda bi, i: (lyr, bi, 0, chunk))
    in_specs = [spec, modspec(layer, 5)] + [spec] * len(ys)
    args = [x, mod4, *ys]
    if probs is not None:
        in_specs.append(pl.BlockSpec((None, tm, len(ys)), lambda bi, i: (bi, i, 0)))
        args.append(probs)
    in_specs.append(pl.BlockSpec((1, d), lambda bi, i: (0, 0)))
    args.append(g_next.reshape(1, d))
    if last:
        out_specs, out_shape = spec, jax.ShapeDtypeStruct((b, t, d), f32)
    else:
        in_specs += [modspec(layer + 1, 0), modspec(layer + 1, 1)]
        args += [mod4, mod4]
        out_specs = [spec, spec]
        out_shape = [jax.ShapeDtypeStruct((b, t, d), f32), jax.ShapeDtypeStruct((b, t, d), bf16)]
    return pl.pallas_call(
        functools.partial(_combine_kernel, n_y=len(ys), weighted=probs is not None, last=last),
        grid=(b, t // tm),
        in_specs=in_specs,
        out_specs=out_specs,
        out_shape=out_shape,
        compiler_params=_cparams("parallel", "parallel"),
        name="ffn_combine",
    )(*args)


def _routing_tables(idx, n_experts, tm):
    n = idx.shape[0]
    na = n * TOP_K
    nt = na // tm + n_experts
    e_flat = idx.reshape(na)
    onehot = (e_flat[:, None] == jnp.arange(n_experts, dtype=jnp.int32)[None, :]).astype(jnp.int32)
    counts = jnp.sum(onehot, axis=0)
    rank = jnp.sum((jnp.cumsum(onehot, axis=0) - 1) * onehot, axis=1)
    tiles_per = (counts + tm - 1) // tm
    tile_end = jnp.cumsum(tiles_per)
    tile_start = tile_end - tiles_per
    slot = tile_start[e_flat] * tm + rank
    n_valid = tile_end[-1:].astype(jnp.int32)
    tile_ids = jnp.arange(nt, dtype=jnp.int32)
    tile_expert = jnp.sum((tile_ids[:, None] >= tile_end[None, :]).astype(jnp.int32), axis=1)
    last_expert = jnp.sum((n_valid - 1 >= tile_end).astype(jnp.int32))
    tile_expert = jnp.where(tile_ids < n_valid, tile_expert, last_expert).astype(jnp.int32)
    tile_rows = jnp.clip(counts[tile_expert] - (tile_ids - tile_start[tile_expert]) * tm, 0, tm)
    tile_quarters = jnp.where(tile_ids < n_valid, (tile_rows + QUARTER - 1) // QUARTER, 0).astype(jnp.int32)
    token = jnp.arange(na, dtype=jnp.int32) // TOP_K
    row_token = jnp.zeros((nt * tm,), jnp.int32).at[slot].set(token)
    return slot.reshape(n, TOP_K), row_token.reshape(nt, 1, tm), tile_expert, tile_quarters


def kernel(x, c, w_mod, b_mod, g_mix, g_ffn, w_in, conv_w, conv_b, conv_ln_g, conv_ln_b, ml_conv_w, ml_conv_b, ml_b_i, ml_b_f, ml_norm_g, attn_norm_g, w_out, ffn_w_gate, ffn_w_up, ffn_w_down, moe_w_router, moe_b_router, moe_w_gate, moe_w_up, moe_w_down, g_final):
    b, t, d = x.shape
    depth = w_mod.shape[0]
    conv_ch = conv_w.shape[-1]
    ml_dim = ml_norm_g.shape[-1]
    ml_heads = ml_b_i.shape[-1]
    n_experts = moe_w_router.shape[-1]
    att_dim = attn_norm_g.shape[-1]
    n_tok = b * t
    tm_ffn = 1024

    mod = _adaln_mod(c, w_mod, b_mod)
    mod4 = mod.reshape(depth, b, 1, 6 * d)

    gate0 = 2 * conv_ch + 4 * ml_dim
    gate1 = gate0 + 2 * ml_heads
    att_col_scale = jnp.concatenate([jnp.full((att_dim,), HEAD_DIM ** -0.5, f32), jnp.ones((2 * att_dim,), f32)])
    w_in_a = w_in[:, :, :gate0]
    w_in_gate = w_in[:, :, gate0:gate1]
    w_in_att = w_in[:, :, gate1:] * att_col_scale[None, None, :]

    h = _prep(x, g_mix[0], mod4, 0, 0)
    out = None
    for l in range(depth):
        z, zg = _in_projection(h.reshape(n_tok, d), w_in_a, w_in_att, w_in_gate, l)
        z = z.reshape(b, t, -1)
        zg = zg.reshape(b, t, -1)
        zg_t = jnp.swapaxes(zg, 1, 2)

        y_conv = _conv_module(z, conv_w[l], conv_b[l], conv_ln_g[l], conv_ln_b[l])
        y_ml = _mlstm(z, zg_t, (2 * conv_ch) // ml_dim, ml_conv_w[l], ml_conv_b[l], ml_b_i[l], ml_b_f[l], ml_norm_g[l])
        y_att = _moba(z, gate0 // HEAD_DIM, attn_norm_g[l])

        j = l // 2
        w_o = w_out[l].astype(bf16)
        if l % 2 == 0:
            x, h = _out_projection(y_conv, y_ml, y_att, w_o, x, mod4, l, g_ffn[l])
            y = _dense_swiglu(h.reshape(n_tok, d), ffn_w_gate, ffn_w_up, ffn_w_down, j, tm=tm_ffn)
            ys, y_probs = [y.reshape(b, t, d)], None
        else:
            x, h32, idx, probs = _out_projection(y_conv, y_ml, y_att, w_o, x, mod4, l, g_ffn[l],
                                                 router=(moe_w_router[j], moe_b_router[j]))
            slot, row_token, tile_expert, tile_quarters = _routing_tables(
                jnp.swapaxes(idx, 1, 2).reshape(n_tok, TOP_K), n_experts, tm_ffn)
            ff = moe_w_gate.shape[-1]
            y_sorted = _moe_swiglu(
                h32.reshape(n_tok, d), row_token, tile_expert + j * n_experts, tile_quarters,
                moe_w_gate.reshape(-1, d, ff), moe_w_up.reshape(-1, d, ff), moe_w_down.reshape(-1, ff, d), tm=tm_ffn)
            ys = [y_sorted.at[slot[:, k]].get(mode="promise_in_bounds").reshape(b, t, d) for k in range(TOP_K)]
            y_probs = jnp.swapaxes(probs, 1, 2)

        if l + 1 < depth:
            x, h = _combine(x, mod4, l, ys, y_probs, g_mix[l + 1], last=False)
        else:
            out = _combine(x, mod4, l, ys, y_probs, g_final, last=True)
    return out
```

```python
import functools

import jax
import jax.numpy as jnp
from jax import lax
from jax.experimental import pallas as pl
from jax.experimental.pallas import tpu as pltpu

f32 = jnp.float32
bf16 = jnp.bfloat16
HIGHEST = lax.Precision.HIGHEST

CONV_GROUPS = 4
HEAD_DIM = 128
ML_CHUNK = 128
MOBA_BLOCK = 256
MOBA_TOPK = 3
TOP_K = 2
EPS = 1e-6
NEG = -1e30

VMEM_LIMIT_BYTES = 56 * 1024 * 1024
LANE = 128


def _cparams(*sem):
    return pltpu.CompilerParams(dimension_semantics=sem, vmem_limit_bytes=VMEM_LIMIT_BYTES)


def _sigmoid(v):
    return 1.0 / (1.0 + jnp.exp(-v))


def _silu(v):
    return v * _sigmoid(v)


def _dot_nt(a, b, **kw):
    return lax.dot_general(a, b, (((1,), (1,)), ((), ())), preferred_element_type=f32, **kw)


def _dot_tn(a, b, **kw):
    return lax.dot_general(a, b, (((0,), (0,)), ((), ())), preferred_element_type=f32, **kw)


def _mod_kernel(c_ref, w_ref, b_ref, o_ref):
    cond = _silu(c_ref[...]).astype(bf16)
    o_ref[...] = jnp.dot(cond, w_ref[...].astype(bf16), preferred_element_type=f32) + b_ref[...]


def _adaln_mod(c, w_mod, b_mod, tn=1024):
    depth, d, n6 = w_mod.shape
    b = c.shape[0]
    return pl.pallas_call(
        _mod_kernel,
        grid=(depth, n6 // tn),
        in_specs=[
            pl.BlockSpec((b, d), lambda l, n: (0, 0)),
            pl.BlockSpec((None, d, tn), lambda l, n: (l, 0, n)),
            pl.BlockSpec((None, 1, tn), lambda l, n: (l, 0, n)),
        ],
        out_specs=pl.BlockSpec((None, b, tn), lambda l, n: (l, 0, n)),
        out_shape=jax.ShapeDtypeStruct((depth, b, n6), f32),
        compiler_params=_cparams("parallel", "parallel"),
        name="adaln_mod",
    )(c, w_mod, b_mod.reshape(depth, 1, n6))


def _norm_modulate(x, g, sh, sc):
    ms = jnp.mean(x * x, axis=-1, keepdims=True)
    h = x * lax.rsqrt(ms + EPS) * g
    return h * (1.0 + sc) + sh


def _prep_kernel(x_ref, g_ref, sh_ref, sc_ref, h_ref):
    h_ref[...] = _norm_modulate(x_ref[...], g_ref[...], sh_ref[...], sc_ref[...]).astype(h_ref.dtype)


def _prep(x, g, mod4, layer, chunk, tm=512):
    b, t, d = x.shape
    return pl.pallas_call(
        _prep_kernel,
        grid=(b, t // tm),
        in_specs=[
            pl.BlockSpec((None, tm, d), lambda bi, i: (bi, i, 0)),
            pl.BlockSpec((1, d), lambda bi, i: (0, 0)),
            pl.BlockSpec((None, None, 1, d), lambda bi, i: (layer, bi, 0, chunk)),
            pl.BlockSpec((None, None, 1, d), lambda bi, i: (layer, bi, 0, chunk + 1)),
        ],
        out_specs=pl.BlockSpec((None, tm, d), lambda bi, i: (bi, i, 0)),
        out_shape=jax.ShapeDtypeStruct((b, t, d), bf16),
        compiler_params=_cparams("parallel", "parallel"),
        name="prep",
    )(x, g.reshape(1, d), mod4, mod4)


def _inproj_kernel(h_ref, wa_ref, wb_ref, wg_ref, z_ref, zg_ref, *, n_a, n_q, q_scale):
    j = pl.program_id(1)

    @pl.when(j == 0)
    def _():
        zg_ref[...] = _dot_nt(h_ref[...], wg_ref[...].astype(bf16))

    @pl.when(j < n_a)
    def _():
        z_ref[...] = _dot_nt(h_ref[...], wa_ref[...].astype(bf16)).astype(bf16)

    @pl.when((j >= n_a) & (j < n_a + n_q))
    def _():
        z_ref[...] = (_dot_nt(h_ref[...], wb_ref[0].astype(bf16)) * q_scale).astype(bf16)

    @pl.when(j >= n_a + n_q)
    def _():
        z_ref[...] = _dot_nt(h_ref[...], wb_ref[0].astype(bf16)).astype(bf16)


def _in_projection(h, w_t, layer, gate0, gate1, q_features, q_scale, tm=2048, tn=512):
    n, d = h.shape
    n_in = w_t.shape[1]
    ng = gate1 - gate0
    n_a = gate0 // tn
    n_b = (n_in - gate1) // tn
    n_q = q_features // tn
    assert n_a * tn == gate0 and n_b * tn == n_in - gate1 and n_q * tn == q_features
    assert gate0 % ng == 0 and gate1 % 8 == 0
    return pl.pallas_call(
        functools.partial(_inproj_kernel, n_a=n_a, n_q=n_q, q_scale=q_scale),
        grid=(n // tm, n_a + n_b),
        in_specs=[
            pl.BlockSpec((tm, d), lambda i, j: (i, 0)),
            pl.BlockSpec((None, tn, d), lambda i, j: (layer, jnp.minimum(j, n_a - 1), 0)),
            pl.BlockSpec((pl.Element(1), pl.Element(tn), pl.Element(d)),
                         lambda i, j: (layer, pl.multiple_of(gate1 + tn * jnp.maximum(j - n_a, 0), 8), 0)),
            pl.BlockSpec((None, ng, d), lambda i, j: (layer, gate0 // ng, 0)),
        ],
        out_specs=[
            pl.BlockSpec((tm, tn), lambda i, j: (i, j)),
            pl.BlockSpec((tm, ng), lambda i, j: (i, 0)),
        ],
        out_shape=[
            jax.ShapeDtypeStruct((n, (n_a + n_b) * tn), bf16),
            jax.ShapeDtypeStruct((n, ng), f32),
        ],
        compiler_params=_cparams("parallel", "arbitrary"),
        name="in_projection",
    )(h, w_t, w_t, w_t)


CONV_PAD = 32
ROWS = 128


def _conv_kernel(ag_ref, w_ref, b_ref, lg_ref, lb_ref, o_ref, xp_ref):
    t, c = o_ref.shape
    taps = w_ref.shape[0]
    gc = c // CONV_GROUPS
    for cg in range(CONV_GROUPS):
        cs = slice(cg * gc, (cg + 1) * gc)
        gs = slice(c + cg * gc, c + (cg + 1) * gc)
        xp_ref[cg, 0:CONV_PAD, :] = jnp.zeros((CONV_PAD, gc), f32)

        def glu(i, _, cg=cg, cs=cs, gs=gs):
            r0 = pl.multiple_of(i * ROWS, ROWS)
            a = ag_ref[pl.ds(r0, ROWS), cs].astype(f32)
            g = ag_ref[pl.ds(r0, ROWS), gs].astype(f32)
            xp_ref[cg, pl.ds(CONV_PAD + r0, ROWS), :] = a * _sigmoid(g)
            return 0

        lax.fori_loop(0, t // ROWS, glu, 0)

        def body(i, _, cg=cg, cs=cs):
            r0 = pl.multiple_of(i * ROWS, ROWS)
            acc = jnp.zeros((ROWS, gc), f32) + b_ref[:, cs]
            for k in range(taps):
                acc = acc + xp_ref[cg, pl.ds(r0 + (CONV_PAD - (taps - 1) + k), ROWS), :] * w_ref[k:k + 1, cs]
            mu = jnp.mean(acc, axis=-1, keepdims=True)
            dv = acc - mu
            var = jnp.mean(dv * dv, axis=-1, keepdims=True)
            y = dv * lax.rsqrt(var + EPS) * lg_ref[:, cs] + lb_ref[:, cs]
            o_ref[pl.ds(r0, ROWS), cs] = _silu(y).astype(o_ref.dtype)
            return 0

        lax.fori_loop(0, t // ROWS, body, 0, unroll=2)


def _conv_module(z, conv_w, conv_b, ln_g, ln_b):
    b, t, _ = z.shape
    taps, c = conv_w.shape
    assert taps - 1 <= CONV_PAD and c // CONV_GROUPS == LANE
    return pl.pallas_call(
        _conv_kernel,
        grid=(b,),
        in_specs=[
            pl.BlockSpec((None, t, 2 * c), lambda bi: (bi, 0, 0)),
            pl.BlockSpec((taps, c), lambda bi: (0, 0)),
            pl.BlockSpec((1, c), lambda bi: (0, 0)),
            pl.BlockSpec((1, c), lambda bi: (0, 0)),
            pl.BlockSpec((1, c), lambda bi: (0, 0)),
        ],
        out_specs=pl.BlockSpec((None, t, c), lambda bi: (bi, 0, 0)),
        out_shape=jax.ShapeDtypeStruct((b, t, c), bf16),
        scratch_shapes=[pltpu.VMEM((CONV_GROUPS, t + CONV_PAD, c // CONV_GROUPS), f32)],
        compiler_params=_cparams("parallel"),
        name="conv_module",
    )(z, conv_w, conv_b.reshape(1, c), ln_g.reshape(1, c), ln_b.reshape(1, c))


ML_PAD = 8


def _mlstm_kernel(q_ref, k_ref, v_ref, o_ref, gr_ref, cw_ref, cb_ref, bic_ref, bfc_ref,
                  ng_ref, y_ref, qp_ref, kp_ref, qs_ref, ks_ref, va_ref, cta_ref, m_ref):
    t, dm = q_ref.shape
    nh = dm // HEAD_DIM
    L = ML_CHUNK
    nc = t // L
    taps = cw_ref.shape[0]
    kscale = HEAD_DIM ** -0.5

    for h in range(nh):
        hs = slice(h * HEAD_DIM, (h + 1) * HEAD_DIM)
        ks_cols = slice(dm + h * HEAD_DIM, dm + (h + 1) * HEAD_DIM)
        qp_ref[h, 0:ML_PAD, :] = jnp.zeros((ML_PAD, HEAD_DIM), f32)
        kp_ref[h, 0:ML_PAD, :] = jnp.zeros((ML_PAD, HEAD_DIM), f32)

        def stage(i, _, h=h, hs=hs):
            r0 = pl.multiple_of(i * ROWS, ROWS)
            qp_ref[h, pl.ds(ML_PAD + r0, ROWS), :] = q_ref[pl.ds(r0, ROWS), hs].astype(f32)
            kp_ref[h, pl.ds(ML_PAD + r0, ROWS), :] = k_ref[pl.ds(r0, ROWS), hs].astype(f32)
            return 0

        lax.fori_loop(0, t // ROWS, stage, 0)

        def sconv(i, _, h=h, hs=hs, ks_cols=ks_cols):
            r0 = pl.multiple_of(i * ROWS, ROWS)
            aq = jnp.zeros((ROWS, HEAD_DIM), f32) + cb_ref[:, hs]
            ak = jnp.zeros((ROWS, HEAD_DIM), f32) + cb_ref[:, ks_cols]
            for j in range(taps):
                off = ML_PAD - (taps - 1) + j
                aq = aq + qp_ref[h, pl.ds(r0 + off, ROWS), :] * cw_ref[j:j + 1, hs]
                ak = ak + kp_ref[h, pl.ds(r0 + off, ROWS), :] * cw_ref[j:j + 1, ks_cols]
            qs_ref[pl.ds(r0, ROWS), hs] = _silu(aq).astype(bf16)
            ks_ref[pl.ds(r0, ROWS), hs] = (_silu(ak) * kscale).astype(bf16)
            return 0

        lax.fori_loop(0, t // ROWS, sconv, 0)

    for h in range(nh):
        hs = slice(h * HEAD_DIM, (h + 1) * HEAD_DIM)
        va_ref[:, 2 * h * HEAD_DIM:(2 * h + 1) * HEAD_DIM] = v_ref[:, hs]
        va_ref[:, (2 * h + 1) * HEAD_DIM:(2 * h + 2) * HEAD_DIM] = jnp.ones((t, HEAD_DIM), bf16)

    ri = lax.broadcasted_iota(jnp.int32, (L, L), 0)
    ci = lax.broadcasted_iota(jnp.int32, (L, L), 1)
    causal = ri >= ci
    lower_f = causal.astype(f32)
    eye_f = (ri == ci).astype(f32)
    eye_b = eye_f.astype(bf16)
    upper_b = (ri <= ci).astype(bf16)
    ones_b = jnp.ones((L, HEAD_DIM), bf16)
    lane = lax.broadcasted_iota(jnp.int32, (nh, L), 1)

    def log_sigmoid(v):
        return jnp.minimum(v, 0.0) - jnp.log(1.0 + jnp.exp(-jnp.abs(v)))

    def split(a, pieces):
        out = []
        for _ in range(pieces - 1):
            p = a.astype(bf16)
            out.append(p)
            a = a - p.astype(f32)
        out.append(a.astype(bf16))
        return out

    def rows_to_columns(weighted):
        return sum(jnp.dot(p, ones_b, preferred_element_type=f32) for p in split(weighted, 2))

    cta_ref[...] = jnp.zeros_like(cta_ref)
    m_ref[...] = jnp.zeros_like(m_ref)

    def chunk(i, _):
        r0 = pl.multiple_of(i * L, L)
        li = gr_ref[0:nh, pl.ds(r0, L)] + bic_ref[...]
        lf = log_sigmoid(gr_ref[nh:2 * nh, pl.ds(r0, L)] + bfc_ref[...])
        b = sum(jnp.dot(p, upper_b, preferred_element_type=f32) for p in split(lf, 3))
        g = li - b
        cmax = g
        for sh in (1, 2, 4, 8, 16, 32, 64):
            cmax = jnp.maximum(cmax, jnp.where(lane >= sh, pltpu.roll(cmax, sh, axis=1), NEG))
        m_prev = m_ref[...]
        big_m = jnp.maximum(cmax, m_prev)
        m_last = big_m[:, L - 1:L]
        b_last = b[:, L - 1:L]
        wk = jnp.exp(g - m_last)
        decay = jnp.exp(m_prev[:, 0:1] - m_last)
        m_ref[...] = jnp.broadcast_to(b_last + m_last, (nh, L))

        heads = range(nh)
        hsl = [slice(h * HEAD_DIM, (h + 1) * HEAD_DIM) for h in heads]
        q = [qs_ref[pl.ds(r0, L), hsl[h]] for h in heads]
        k = [ks_ref[pl.ds(r0, L), hsl[h]] for h in heads]
        va = [va_ref[pl.ds(r0, L), 2 * h * HEAD_DIM:(2 * h + 2) * HEAD_DIM] for h in heads]
        cta = [cta_ref[h] for h in heads]
        m_col = [rows_to_columns(eye_f * big_m[h:h + 1, :]) for h in heads]
        b_col = [rows_to_columns(lower_f * lf[h:h + 1, :]) for h in heads]
        qk = [_dot_nt(q[h], k[h]) for h in heads]
        k_t = [_dot_nt(eye_b, k[h]) for h in heads]
        inter = [jnp.dot(q[h], cta[h].astype(bf16), preferred_element_type=f32) for h in heads]
        s = [(qk[h] * jnp.exp(jnp.where(causal, g[h:h + 1, :] - m_col[h], NEG))).astype(bf16) for h in heads]
        kw_t = [(k_t[h] * wk[h:h + 1, :]).astype(bf16) for h in heads]
        w_inter = [jnp.exp(m_prev[h:h + 1, 0:1] - m_col[h]) for h in heads]
        both = [jnp.dot(s[h], va[h], preferred_element_type=f32)
                + jnp.concatenate([w_inter[h], w_inter[h]], axis=1) * inter[h] for h in heads]
        new_cta = [decay[h:h + 1, :] * cta[h] + jnp.dot(kw_t[h], va[h], preferred_element_type=f32) for h in heads]
        hout = [both[h][:, 0:HEAD_DIM] / jnp.maximum(jnp.abs(both[h][:, HEAD_DIM:2 * HEAD_DIM]),
                                                     jnp.exp(-(b_col[h] + m_col[h]))) for h in heads]
        ssq = [sum(jnp.dot(pc, ones_b, preferred_element_type=f32) for pc in split(hout[h] * hout[h], 2)) for h in heads]
        for h in heads:
            hn = hout[h] * lax.rsqrt(ssq[h] * (1.0 / HEAD_DIM) + EPS) * ng_ref[:, hsl[h]]
            og = _sigmoid(o_ref[pl.ds(r0, L), hsl[h]].astype(f32))
            y_ref[pl.ds(r0, L), hsl[h]] = (og * hn).astype(y_ref.dtype)
            cta_ref[h] = new_cta[h]
        return 0

    lax.fori_loop(0, nc, chunk, 0, unroll=2)


def _mlstm(z, zg_t, col0, ml_conv_w, ml_conv_b, b_i, b_f, norm_g):
    b, t, _ = z.shape
    dm = norm_g.shape[0]
    nh = dm // HEAD_DIM
    taps = ml_conv_w.shape[0]
    assert taps - 1 <= ML_PAD

    def zspec(j):
        return pl.BlockSpec((None, t, dm), lambda bi: (bi, 0, col0 + j))

    return pl.pallas_call(
        _mlstm_kernel,
        grid=(b,),
        in_specs=[
            zspec(0), zspec(1), zspec(2), zspec(3),
            pl.BlockSpec((None, 2 * nh, t), lambda bi: (bi, 0, 0)),
            pl.BlockSpec((taps, 2 * dm), lambda bi: (0, 0)),
            pl.BlockSpec((1, 2 * dm), lambda bi: (0, 0)),
            pl.BlockSpec((nh, 1), lambda bi: (0, 0)),
            pl.BlockSpec((nh, 1), lambda bi: (0, 0)),
            pl.BlockSpec((1, dm), lambda bi: (0, 0)),
        ],
        out_specs=pl.BlockSpec((None, t, dm), lambda bi: (bi, 0, 0)),
        out_shape=jax.ShapeDtypeStruct((b, t, dm), bf16),
        scratch_shapes=[
            pltpu.VMEM((nh, t + ML_PAD, HEAD_DIM), f32),
            pltpu.VMEM((nh, t + ML_PAD, HEAD_DIM), f32),
            pltpu.VMEM((t, dm), bf16),
            pltpu.VMEM((t, dm), bf16),
            pltpu.VMEM((t, 2 * dm), bf16),
            pltpu.VMEM((nh, HEAD_DIM, 2 * HEAD_DIM), f32),
            pltpu.VMEM((nh, ML_CHUNK), f32),
        ],
        compiler_params=_cparams("parallel"),
        name="mlstm",
    )(z, z, z, z, zg_t, ml_conv_w, ml_conv_b.reshape(1, 2 * dm), b_i.reshape(nh, 1), b_f.reshape(nh, 1),
      norm_g.reshape(1, dm))


MASK_BIG = 1e30
BIAS_COLS = 128


def _moba_kernel(q_ref, k_ref, v_ref, kx_ref, sl_ref, ng_ref, y_ref, km_ref, s_ref, kc_ref, vc_ref):
    t, dh = q_ref.shape
    blk = MOBA_BLOCK
    nb = t // blk

    kc_ref[:, 0:dh] = k_ref[...]
    kc_ref[:, dh:dh + BIAS_COLS] = kx_ref[...]
    vc_ref[:, 0:dh] = v_ref[...]
    vc_ref[:, dh:2 * dh] = jnp.ones((t, dh), bf16)

    for n in range(nb):
        km_ref[n:n + 1, :] = jnp.mean(k_ref[n * blk:(n + 1) * blk, :].astype(f32), axis=0, keepdims=True)
    gate_t = _dot_nt(km_ref[...], q_ref[...].astype(f32), precision=HIGHEST)

    blk_id = lax.broadcasted_iota(jnp.int32, (nb, blk), 0)
    ri = lax.broadcasted_iota(jnp.int32, (blk, blk), 0)
    ci = lax.broadcasted_iota(jnp.int32, (blk, blk), 1)
    causal = ri >= ci
    eye = (ri == ci).astype(bf16)
    lane = lax.broadcasted_iota(jnp.int32, (blk, BIAS_COLS), 1)
    rowf = lax.broadcasted_iota(jnp.int32, (blk, BIAS_COLS), 0).astype(f32)
    slope = sl_ref[:, 0:1]

    def logits_pass(j):
        qj = q_ref[j * blk:(j + 1) * blk, :]
        masked_t = jnp.zeros((nb, blk), f32)
        if j > 0:
            g = jnp.where(blk_id < j, gate_t[:, j * blk:(j + 1) * blk], NEG)
            for n in range(j):
                gn = g[n:n + 1, :]
                ahead = (g > gn) | ((g == gn) & (blk_id < n))
                rank = jnp.sum(jnp.where(ahead, 1.0, 0.0), axis=0, keepdims=True)
                masked_t = jnp.where((blk_id == n) & (rank >= float(MOBA_TOPK)), -1.0, masked_t)
        masked_pad = jnp.concatenate([masked_t, jnp.zeros((BIAS_COLS - nb, blk), f32)], axis=0).astype(bf16)
        qx = _dot_nt(eye, masked_pad)
        qx = jnp.where(lane == nb, -slope * float(blk * j), qx)
        qx = jnp.where(lane == nb + 1, -slope * rowf, qx)
        qx = jnp.where((lane == nb + 2) | (lane == nb + 3), 1.0, qx).astype(bf16)
        qc = jnp.concatenate([qj, qx], axis=1)
        mx = None
        for n in range(j + 1):
            ks = slice(n * blk, (n + 1) * blk)
            lg = _dot_nt(qc, kc_ref[ks, :])
            if n == j:
                lg = jnp.where(causal, lg, NEG)
            s_ref[j % 2, :, ks] = lg
            half = jnp.maximum(lg[:, 0:LANE], lg[:, LANE:2 * LANE])
            mx = half if mx is None else jnp.maximum(mx, half)
        return jnp.max(mx, axis=-1, keepdims=True)

    def value_pass(j, m):
        acc = jnp.zeros((blk, 2 * dh), f32)
        for n in range(j + 1):
            ks = slice(n * blk, (n + 1) * blk)
            p = jnp.exp((s_ref[j % 2, :, ks] - m).astype(bf16))
            acc = acc + jnp.dot(p, vc_ref[ks, :], preferred_element_type=f32)
        o = acc[:, 0:dh] / acc[:, dh:2 * dh]
        o = o * lax.rsqrt(jnp.mean(o * o, axis=-1, keepdims=True) + EPS) * ng_ref[...]
        y_ref[j * blk:(j + 1) * blk, :] = o.astype(y_ref.dtype)

    m_prev = logits_pass(0)
    for j in range(1, nb):
        m_cur = logits_pass(j)
        value_pass(j - 1, m_prev)
        m_prev = m_cur
    value_pass(nb - 1, m_prev)


def _moba(z, col0, norm_g):
    b, t, _ = z.shape
    dh = HEAD_DIM
    nh = norm_g.shape[0] // dh
    blk = MOBA_BLOCK
    assert t % blk == 0
    nb = t // blk
    assert nb + 4 <= BIAS_COLS
    slopes = 2.0 ** (-(8.0 / nh) * jnp.arange(1, nh + 1, dtype=f32))
    pos = jnp.arange(t, dtype=jnp.int32)
    col = jnp.arange(BIAS_COLS, dtype=jnp.int32)[None, :]
    kblk = (pos // blk)[:, None]
    base = jnp.where(col == kblk, MASK_BIG, 0.0) + jnp.where((col == nb) | (col == nb + 1), 1.0, 0.0)
    per_head = (jnp.where(col == nb + 2, (kblk * blk).astype(f32), 0.0)
                + jnp.where(col == nb + 3, (pos % blk).astype(f32)[:, None], 0.0))
    kx = (base[None] + slopes[:, None, None] * per_head[None]).astype(bf16)
    slope_rows = jnp.broadcast_to(slopes[:, None, None], (nh, 1, LANE))

    def zspec(j):
        return pl.BlockSpec((None, t, dh), lambda bi, h: (bi, 0, col0 + j * nh + h))

    return pl.pallas_call(
        _moba_kernel,
        grid=(b, nh),
        in_specs=[
            zspec(0), zspec(1), zspec(2),
            pl.BlockSpec((None, t, BIAS_COLS), lambda bi, h: (h, 0, 0)),
            pl.BlockSpec((None, 1, LANE), lambda bi, h: (h, 0, 0)),
            pl.BlockSpec((None, 1, dh), lambda bi, h: (h, 0, 0)),
        ],
        out_specs=pl.BlockSpec((None, t, dh), lambda bi, h: (bi, 0, h)),
        out_shape=jax.ShapeDtypeStruct((b, t, nh * dh), bf16),
        scratch_shapes=[pltpu.VMEM((nb, dh), f32), pltpu.VMEM((2, blk, t), f32), pltpu.VMEM((t, dh + BIAS_COLS), bf16),
                        pltpu.VMEM((t, 2 * dh), bf16)],
        compiler_params=_cparams("parallel", "parallel"),
        name="moba_attention",
    )(z, z, z, kx, slope_rows, norm_g.reshape(nh, 1, dh))


def _router_logits_t(w_t32, h32, hb):
    h_lo = (h32 - hb.astype(f32)).astype(bf16)
    w_hi = w_t32.astype(bf16)
    w_lo = (w_t32 - w_hi.astype(f32)).astype(bf16)
    return _dot_nt(w_hi, hb) + _dot_nt(w_hi, h_lo) + _dot_nt(w_lo, hb)


def _top2_t(logits):
    ne = logits.shape[0]
    eid = lax.broadcasted_iota(jnp.int32, logits.shape, 0)
    m1 = jnp.max(logits, axis=0, keepdims=True)
    i1 = jnp.min(jnp.where(logits == m1, eid, ne), axis=0, keepdims=True)
    rest = jnp.where(eid == i1, -jnp.inf, logits)
    m2 = jnp.max(rest, axis=0, keepdims=True)
    i2 = jnp.min(jnp.where(rest == m2, eid, ne), axis=0, keepdims=True)
    e2 = jnp.exp(m2 - m1)
    p1 = 1.0 / (1.0 + e2)
    return i1, i2, p1, e2 * p1


OUTPROJ_SPLIT = 2


def _outproj_kernel(yc_ref, ym_ref, ya_ref, w_ref, x_ref, gt_ref, g_ref, sh_ref, sc_ref, *rest, routed):
    if routed:
        wr_ref, br_ref, xo_ref, h_ref, idx_ref, p_ref = rest
    else:
        xo_ref, h_ref = rest
    tm = x_ref.shape[0]
    sub = tm // OUTPROJ_SPLIT
    blocks = [slice(r * sub, (r + 1) * sub) for r in range(OUTPROJ_SPLIT)]
    accs = [jnp.dot(jnp.concatenate([yc_ref[rs, :], ym_ref[rs, :], ya_ref[rs, :]], axis=1), w_ref[...],
                    preferred_element_type=f32) for rs in blocks]
    for rs, acc in zip(blocks, accs):
        xn = x_ref[rs, :] + gt_ref[...] * acc
        xo_ref[rs, :] = xn
        h = _norm_modulate(xn, g_ref[...], sh_ref[...], sc_ref[...])
        hb = h.astype(bf16)
        h_ref[rs, :] = h if routed else hb
        if routed:
            i1, i2, p1, p2 = _top2_t(_router_logits_t(wr_ref[...], h, hb) + br_ref[...])
            first = lax.broadcasted_iota(jnp.int32, (TOP_K, sub), 0) == 0
            idx_ref[:, rs] = jnp.where(first, i1, i2)
            p_ref[:, rs] = jnp.where(first, p1, p2)


def _out_projection(y_conv, y_ml, y_att, w_out, x, mod4, layer, g_ffn, router=None, tm=512):
    b, t, d = x.shape
    c1, c2, c3 = y_conv.shape[-1], y_ml.shape[-1], y_att.shape[-1]
    assert c1 + c2 + c3 == w_out.shape[0]
    row = lambda c: pl.BlockSpec((None, tm, c), lambda bi, i: (bi, i, 0))
    modspec = lambda chunk: pl.BlockSpec((None, None, 1, d), lambda bi, i: (layer, bi, 0, chunk))
    in_specs = [
        row(c1), row(c2), row(c3),
        pl.BlockSpec(w_out.shape, lambda bi, i: (0, 0)),
        row(d), modspec(2),
        pl.BlockSpec((1, d), lambda bi, i: (0, 0)), modspec(3), modspec(4),
    ]
    args = [y_conv, y_ml, y_att, w_out, x, mod4, g_ffn.reshape(1, d), mod4, mod4]
    out_specs = [row(d), row(d)]
    out_shape = [jax.ShapeDtypeStruct((b, t, d), f32), jax.ShapeDtypeStruct((b, t, d), bf16 if router is None else f32)]
    if router is not None:
        w_r, b_r = router
        ne = w_r.shape[1]
        in_specs += [pl.BlockSpec((ne, d), lambda bi, i: (0, 0)), pl.BlockSpec((ne, 1), lambda bi, i: (0, 0))]
        args += [w_r.T, b_r.reshape(ne, 1)]
        kspec = pl.BlockSpec((None, TOP_K, tm), lambda bi, i: (bi, 0, i))
        out_specs += [kspec, kspec]
        out_shape += [jax.ShapeDtypeStruct((b, TOP_K, t), jnp.int32), jax.ShapeDtypeStruct((b, TOP_K, t), f32)]
    return pl.pallas_call(
        functools.partial(_outproj_kernel, routed=router is not None),
        grid=(b, t // tm),
        in_specs=in_specs,
        out_specs=out_specs,
        out_shape=out_shape,
        compiler_params=_cparams("parallel", "parallel"),
        name="out_projection",
    )(*args)


def _swiglu_rows(hb, wg_ref, wu_ref, wd_ref):
    g = jnp.dot(hb, wg_ref[...].astype(bf16), preferred_element_type=f32)
    u = jnp.dot(hb, wu_ref[...].astype(bf16), preferred_element_type=f32)
    a = (_silu(g) * u).astype(bf16)
    return jnp.dot(a, wd_ref[...].astype(bf16), preferred_element_type=f32)


def _dense_swiglu_kernel(h_ref, wg_ref, wu_ref, wd_ref, o_ref, acc_ref):
    f = pl.program_id(1)

    @pl.when(f == 0)
    def _():
        acc_ref[...] = jnp.zeros_like(acc_ref)

    acc_ref[...] += _swiglu_rows(h_ref[...], wg_ref, wu_ref, wd_ref)

    @pl.when(f == pl.num_programs(1) - 1)
    def _():
        o_ref[...] = acc_ref[...].astype(o_ref.dtype)


def _dense_swiglu(h, wg, wu, wd, layer, tm=1024, tf=512):
    n, d = h.shape
    ff = wg.shape[-1]
    return pl.pallas_call(
        _dense_swiglu_kernel,
        grid=(n // tm, ff // tf),
        in_specs=[
            pl.BlockSpec((tm, d), lambda i, f: (i, 0)),
            pl.BlockSpec((None, d, tf), lambda i, f: (layer, 0, f)),
            pl.BlockSpec((None, d, tf), lambda i, f: (layer, 0, f)),
            pl.BlockSpec((None, tf, d), lambda i, f: (layer, f, 0)),
        ],
        out_specs=pl.BlockSpec((tm, d), lambda i, f: (i, 0)),
        out_shape=jax.ShapeDtypeStruct((n, d), bf16),
        scratch_shapes=[pltpu.VMEM((tm, d), f32)],
        compiler_params=_cparams("parallel", "arbitrary"),
        name="dense_swiglu",
    )(h, wg, wu, wd)


QUARTER = 256
GATHER_UNROLL = 8


def _moe_swiglu_kernel(te_ref, nq_ref, tok_ref, tok_next_ref, h_hbm, wg_ref, wu_ref, wd_ref, o_ref,
                       hrows_ref, hb_ref, acc_ref, sem):
    i = pl.program_id(0)
    f = pl.program_id(1)
    nt = pl.num_programs(0)
    nq = nq_ref[i]
    tm = acc_ref.shape[0]

    def row_copy(src_row, dst_row, rows=1):
        return pltpu.make_async_copy(h_hbm.at[pl.ds(src_row, rows)], hrows_ref.at[pl.ds(dst_row, rows)], sem)

    def start_gather(tokens_ref, quarters):
        def issue(g, _):
            for u in range(GATHER_UNROLL):
                r = g * GATHER_UNROLL + u
                row_copy(tokens_ref[0, r], r).start()
            return 0

        lax.fori_loop(0, quarters * (QUARTER // GATHER_UNROLL), issue, 0)

    @pl.when(f == 0)
    def _():
        acc_ref[...] = jnp.zeros_like(acc_ref)

        @pl.when((i == 0) & (nq > 0))
        def _():
            start_gather(tok_ref, nq)

        @pl.when(nq > 0)
        def _():
            row_copy(0, 0, nq * QUARTER).wait()
            for q in range(tm // QUARTER):
                @pl.when(q < nq)
                def _(q=q):
                    qs = slice(q * QUARTER, (q + 1) * QUARTER)
                    hb_ref[qs, :] = hrows_ref[qs, :].astype(bf16)

        nq_next = nq_ref[jnp.minimum(i + 1, nt - 1)]

        @pl.when((i + 1 < nt) & (nq_next > 0))
        def _():
            start_gather(tok_next_ref, nq_next)

    for q in range(1, tm // QUARTER + 1):
        @pl.when(nq == q)
        def _(q=q):
            rows = q * QUARTER
            acc_ref[0:rows, :] += _swiglu_rows(hb_ref[0:rows, :], wg_ref, wu_ref, wd_ref)

    @pl.when(f == pl.num_programs(1) - 1)
    def _():
        o_ref[...] = acc_ref[...].astype(o_ref.dtype)


def _moe_swiglu(h32, row_token, tile_expert, tile_quarters, wg, wu, wd, tm=1024, tf=256):
    n, d = h32.shape
    nt = row_token.shape[0]
    ff = wg.shape[-1]
    nf = ff // tf
    assert tm % QUARTER == 0

    def wmap(i, f, te, nq):
        return (te[i], 0, jnp.where(nq[i] > 0, f, nf - 1))

    def wdmap(i, f, te, nq):
        return (te[i], jnp.where(nq[i] > 0, f, nf - 1), 0)

    grid_spec = pltpu.PrefetchScalarGridSpec(
        num_scalar_prefetch=2,
        grid=(nt, nf),
        in_specs=[
            pl.BlockSpec((None, 1, tm), lambda i, f, te, nq: (i, 0, 0), memory_space=pltpu.SMEM),
            pl.BlockSpec((None, 1, tm), lambda i, f, te, nq: (jnp.minimum(i + 1, nt - 1), 0, 0),
                         memory_space=pltpu.SMEM),
            pl.BlockSpec(memory_space=pl.ANY),
            pl.BlockSpec((None, d, tf), wmap),
            pl.BlockSpec((None, d, tf), wmap),
            pl.BlockSpec((None, tf, d), wdmap),
        ],
        out_specs=pl.BlockSpec((tm, d), lambda i, f, te, nq: (i, 0)),
        scratch_shapes=[
            pltpu.VMEM((tm, d), f32),
            pltpu.VMEM((tm, d), bf16),
            pltpu.VMEM((tm, d), f32),
            pltpu.SemaphoreType.DMA,
        ],
    )
    return pl.pallas_call(
        _moe_swiglu_kernel,
        grid_spec=grid_spec,
        out_shape=jax.ShapeDtypeStruct((nt * tm, d), bf16),
        compiler_params=_cparams("arbitrary", "arbitrary"),
        name="moe_swiglu",
    )(tile_expert, tile_quarters, row_token, row_token, h32, wg, wu, wd)


def _combine_kernel(*refs, n_y, weighted, last):
    x_ref, gt_ref = refs[0], refs[1]
    ys = [r[...].astype(f32) for r in refs[2:2 + n_y]]
    rest = refs[2 + n_y:]
    if weighted:
        p = rest[0][...]
        rest = rest[1:]
        ys = [p[:, k:k + 1] * y for k, y in enumerate(ys)]
    y = ys[0]
    for yk in ys[1:]:
        y = y + yk
    xn = x_ref[...] + gt_ref[...] * y
    if last:
        g_ref, o_ref = rest
        o_ref[...] = xn * lax.rsqrt(jnp.mean(xn * xn, axis=-1, keepdims=True) + EPS) * g_ref[...]
    else:
        g_ref, sh_ref, sc_ref, xo_ref, h_ref = rest
        xo_ref[...] = xn
        h_ref[...] = _norm_modulate(xn, g_ref[...], sh_ref[...], sc_ref[...]).astype(h_ref.dtype)


def _combine(x, mod4, layer, ys, probs, g_next, last, tm=512):
    b, t, d = x.shape
    spec = pl.BlockSpec((None, tm, d), lambda bi, i: (bi, i, 0))
    modspec = lambda lyr, chunk: pl.BlockSpec((None, None, 1, d), lambda bi, i: (lyr, bi, 0, chunk))
    in_specs = [spec, modspec(layer, 5)] + [spec] * len(ys)
    args = [x, mod4, *ys]
    if probs is not None:
        in_specs.append(pl.BlockSpec((None, tm, len(ys)), lambda bi, i: (bi, i, 0)))
        args.append(probs)
    in_specs.append(pl.BlockSpec((1, d), lambda bi, i: (0, 0)))
    args.append(g_next.reshape(1, d))
    if last:
        out_specs, out_shape = spec, jax.ShapeDtypeStruct((b, t, d), f32)
    else:
        in_specs += [modspec(layer + 1, 0), modspec(layer + 1, 1)]
        args += [mod4, mod4]
        out_specs = [spec, spec]
        out_shape = [jax.ShapeDtypeStruct((b, t, d), f32), jax.ShapeDtypeStruct((b, t, d), bf16)]
    return pl.pallas_call(
        functools.partial(_combine_kernel, n_y=len(ys), weighted=probs is not None, last=last),
        grid=(b, t // tm),
        in_specs=in_specs,
        out_specs=out_specs,
        out_shape=out_shape,
        compiler_params=_cparams("parallel", "parallel"),
        name="ffn_combine",
    )(*args)


def _routing_tables(idx, n_experts, tm):
    n = idx.shape[0]
    na = n * TOP_K
    nt = na // tm + n_experts
    e_flat = idx.reshape(na)
    onehot = (e_flat[:, None] == jnp.arange(n_experts, dtype=jnp.int32)[None, :]).astype(jnp.int32)
    counts = jnp.sum(onehot, axis=0)
    rank = jnp.sum((jnp.cumsum(onehot, axis=0) - 1) * onehot, axis=1)
    tiles_per = (counts + tm - 1) // tm
    tile_end = jnp.cumsum(tiles_per)
    tile_start = tile_end - tiles_per
    slot = tile_start[e_flat] * tm + rank
    n_valid = tile_end[-1:].astype(jnp.int32)
    tile_ids = jnp.arange(nt, dtype=jnp.int32)
    tile_expert = jnp.sum((tile_ids[:, None] >= tile_end[None, :]).astype(jnp.int32), axis=1)
    last_expert = jnp.sum((n_valid - 1 >= tile_end).astype(jnp.int32))
    tile_expert = jnp.where(tile_ids < n_valid, tile_expert, last_expert).astype(jnp.int32)
    tile_rows = jnp.clip(counts[tile_expert] - (tile_ids - tile_start[tile_expert]) * tm, 0, tm)
    tile_quarters = jnp.where(tile_ids < n_valid, (tile_rows + QUARTER - 1) // QUARTER, 0).astype(jnp.int32)
    token = jnp.arange(na, dtype=jnp.int32) // TOP_K
    row_token = jnp.zeros((nt * tm,), jnp.int32).at[slot].set(token)
    return slot.reshape(n, TOP_K), row_token.reshape(nt, 1, tm), tile_expert, tile_quarters


def kernel(x, c, w_mod, b_mod, g_mix, g_ffn, w_in, conv_w, conv_b, conv_ln_g, conv_ln_b, ml_conv_w, ml_conv_b, ml_b_i, ml_b_f, ml_norm_g, attn_norm_g, w_out, ffn_w_gate, ffn_w_up, ffn_w_down, moe_w_router, moe_b_router, moe_w_gate, moe_w_up, moe_w_down, g_final):
    b, t, d = x.shape
    depth = w_mod.shape[0]
    conv_ch = conv_w.shape[-1]
    ml_dim = ml_norm_g.shape[-1]
    ml_heads = ml_b_i.shape[-1]
    n_experts = moe_w_router.shape[-1]
    att_dim = attn_norm_g.shape[-1]
    n_tok = b * t
    tm_ffn = 1024

    mod = _adaln_mod(c, w_mod, b_mod)
    mod4 = mod.reshape(depth, b, 1, 6 * d)

    gate0 = 2 * conv_ch + 4 * ml_dim
    gate1 = gate0 + 2 * ml_heads
    w_in_t = jnp.swapaxes(w_in, 1, 2)

    h = _prep(x, g_mix[0], mod4, 0, 0)
    out = None
    for l in range(depth):
        z, zg = _in_projection(h.reshape(n_tok, d), w_in_t, l, gate0, gate1, att_dim, HEAD_DIM ** -0.5)
        z = z.reshape(b, t, -1)
        zg = zg.reshape(b, t, -1)
        zg_t = jnp.swapaxes(zg, 1, 2)

        y_conv = _conv_module(z, conv_w[l], conv_b[l], conv_ln_g[l], conv_ln_b[l])
        y_ml = _mlstm(z, zg_t, (2 * conv_ch) // ml_dim, ml_conv_w[l], ml_conv_b[l], ml_b_i[l], ml_b_f[l], ml_norm_g[l])
        y_att = _moba(z, gate0 // HEAD_DIM, attn_norm_g[l])

        j = l // 2
        w_o = w_out[l].astype(bf16)
        if l % 2 == 0:
            x, h = _out_projection(y_conv, y_ml, y_att, w_o, x, mod4, l, g_ffn[l])
            y = _dense_swiglu(h.reshape(n_tok, d), ffn_w_gate, ffn_w_up, ffn_w_down, j, tm=tm_ffn)
            ys, y_probs = [y.reshape(b, t, d)], None
        else:
            x, h32, idx, probs = _out_projection(y_conv, y_ml, y_att, w_o, x, mod4, l, g_ffn[l],
                                                 router=(moe_w_router[j], moe_b_router[j]))
            slot, row_token, tile_expert, tile_quarters = _routing_tables(
                jnp.swapaxes(idx, 1, 2).reshape(n_tok, TOP_K), n_experts, tm_ffn)
            ff = moe_w_gate.shape[-1]
            y_sorted = _moe_swiglu(
                h32.reshape(n_tok, d), row_token, tile_expert + j * n_experts, tile_quarters,
                moe_w_gate.reshape(-1, d, ff), moe_w_up.reshape(-1, d, ff), moe_w_down.reshape(-1, ff, d), tm=tm_ffn)
            ys = [y_sorted.at[slot[:, k]].get(mode="promise_in_bounds").reshape(b, t, d) for k in range(TOP_K)]
            y_probs = jnp.swapaxes(probs, 1, 2)

        if l + 1 < depth:
            x, h = _combine(x, mod4, l, ys, y_probs, g_mix[l + 1], last=False)
        else:
            out = _combine(x, mod4, l, ys, y_probs, g_final, last=True)
    return out
```

```python
import functools

import jax
import jax.numpy as jnp
from jax import lax
from jax.experimental import pallas as pl
from jax.experimental.pallas import tpu as pltpu

f32 = jnp.float32
bf16 = jnp.bfloat16
HIGHEST = lax.Precision.HIGHEST

CONV_GROUPS = 4
HEAD_DIM = 128
ML_CHUNK = 128
MOBA_BLOCK = 256
MOBA_TOPK = 3
TOP_K = 2
EPS = 1e-6
NEG = -1e30

VMEM_LIMIT_BYTES = 56 * 1024 * 1024
LANE = 128


def _cparams(*sem):
    return pltpu.CompilerParams(dimension_semantics=sem, vmem_limit_bytes=VMEM_LIMIT_BYTES)


def _sigmoid(v):
    return 1.0 / (1.0 + jnp.exp(-v))


def _silu(v):
    return v * _sigmoid(v)


def _dot_nt(a, b, **kw):
    return lax.dot_general(a, b, (((1,), (1,)), ((), ())), preferred_element_type=f32, **kw)


def _dot_tn(a, b, **kw):
    return lax.dot_general(a, b, (((0,), (0,)), ((), ())), preferred_element_type=f32, **kw)


def _mod_kernel(c_ref, w_ref, b_ref, o_ref):
    cond = _silu(c_ref[...]).astype(bf16)
    o_ref[...] = jnp.dot(cond, w_ref[...].astype(bf16), preferred_element_type=f32) + b_ref[...]


def _adaln_mod(c, w_mod, b_mod, tn=1024):
    depth, d, n6 = w_mod.shape
    b = c.shape[0]
    return pl.pallas_call(
        _mod_kernel,
        grid=(depth, n6 // tn),
        in_specs=[
            pl.BlockSpec((b, d), lambda l, n: (0, 0)),
            pl.BlockSpec((None, d, tn), lambda l, n: (l, 0, n)),
            pl.BlockSpec((None, 1, tn), lambda l, n: (l, 0, n)),
        ],
        out_specs=pl.BlockSpec((None, b, tn), lambda l, n: (l, 0, n)),
        out_shape=jax.ShapeDtypeStruct((depth, b, n6), f32),
        compiler_params=_cparams("parallel", "parallel"),
        name="adaln_mod",
    )(c, w_mod, b_mod.reshape(depth, 1, n6))


def _norm_modulate(x, g, sh, sc):
    ms = jnp.mean(x * x, axis=-1, keepdims=True)
    h = x * lax.rsqrt(ms + EPS) * g
    return h * (1.0 + sc) + sh


def _prep_kernel(x_ref, g_ref, sh_ref, sc_ref, h_ref):
    h_ref[...] = _norm_modulate(x_ref[...], g_ref[...], sh_ref[...], sc_ref[...]).astype(h_ref.dtype)


def _prep(x, g, mod4, layer, chunk, tm=512):
    b, t, d = x.shape
    return pl.pallas_call(
        _prep_kernel,
        grid=(b, t // tm),
        in_specs=[
            pl.BlockSpec((None, tm, d), lambda bi, i: (bi, i, 0)),
            pl.BlockSpec((1, d), lambda bi, i: (0, 0)),
            pl.BlockSpec((None, None, 1, d), lambda bi, i: (layer, bi, 0, chunk)),
            pl.BlockSpec((None, None, 1, d), lambda bi, i: (layer, bi, 0, chunk + 1)),
        ],
        out_specs=pl.BlockSpec((None, tm, d), lambda bi, i: (bi, i, 0)),
        out_shape=jax.ShapeDtypeStruct((b, t, d), bf16),
        compiler_params=_cparams("parallel", "parallel"),
        name="prep",
    )(x, g.reshape(1, d), mod4, mod4)


def _inproj_kernel(h_ref, wa_ref, wb_ref, wg_ref, z_ref, zg_ref, *, n_a, n_q, q_scale):
    j = pl.program_id(1)

    @pl.when(j == 0)
    def _():
        zg_ref[...] = _dot_nt(h_ref[...], wg_ref[...].astype(bf16))

    @pl.when(j < n_a)
    def _():
        z_ref[...] = _dot_nt(h_ref[...], wa_ref[...].astype(bf16)).astype(bf16)

    @pl.when((j >= n_a) & (j < n_a + n_q))
    def _():
        z_ref[...] = (_dot_nt(h_ref[...], wb_ref[0].astype(bf16)) * q_scale).astype(bf16)

    @pl.when(j >= n_a + n_q)
    def _():
        z_ref[...] = _dot_nt(h_ref[...], wb_ref[0].astype(bf16)).astype(bf16)


def _in_projection(h, w_t, layer, gate0, gate1, q_features, q_scale, tm=2048, tn=512):
    n, d = h.shape
    n_in = w_t.shape[1]
    ng = gate1 - gate0
    n_a = gate0 // tn
    n_b = (n_in - gate1) // tn
    n_q = q_features // tn
    assert n_a * tn == gate0 and n_b * tn == n_in - gate1 and n_q * tn == q_features
    assert gate0 % ng == 0 and gate1 % 8 == 0
    return pl.pallas_call(
        functools.partial(_inproj_kernel, n_a=n_a, n_q=n_q, q_scale=q_scale),
        grid=(n // tm, n_a + n_b),
        in_specs=[
            pl.BlockSpec((tm, d), lambda i, j: (i, 0)),
            pl.BlockSpec((None, tn, d), lambda i, j: (layer, jnp.minimum(j, n_a - 1), 0)),
            pl.BlockSpec((pl.Element(1), pl.Element(tn), pl.Element(d)),
                         lambda i, j: (layer, pl.multiple_of(gate1 + tn * jnp.maximum(j - n_a, 0), 8), 0)),
            pl.BlockSpec((None, ng, d), lambda i, j: (layer, gate0 // ng, 0)),
        ],
        out_specs=[
            pl.BlockSpec((tm, tn), lambda i, j: (i, j)),
            pl.BlockSpec((tm, ng), lambda i, j: (i, 0)),
        ],
        out_shape=[
            jax.ShapeDtypeStruct((n, (n_a + n_b) * tn), bf16),
            jax.ShapeDtypeStruct((n, ng), f32),
        ],
        compiler_params=_cparams("parallel", "arbitrary"),
        name="in_projection",
    )(h, w_t, w_t, w_t)


CONV_PAD = 32
ROWS = 128


def _conv_kernel(ag_ref, w_ref, b_ref, lg_ref, lb_ref, o_ref, xp_ref):
    t, c = o_ref.shape
    taps = w_ref.shape[0]
    gc = c // CONV_GROUPS
    for cg in range(CONV_GROUPS):
        cs = slice(cg * gc, (cg + 1) * gc)
        gs = slice(c + cg * gc, c + (cg + 1) * gc)
        xp_ref[cg, 0:CONV_PAD, :] = jnp.zeros((CONV_PAD, gc), f32)

        def glu(i, _, cg=cg, cs=cs, gs=gs):
            r0 = pl.multiple_of(i * ROWS, ROWS)
            a = ag_ref[pl.ds(r0, ROWS), cs].astype(f32)
            g = ag_ref[pl.ds(r0, ROWS), gs].astype(f32)
            xp_ref[cg, pl.ds(CONV_PAD + r0, ROWS), :] = a * _sigmoid(g)
            return 0

        lax.fori_loop(0, t // ROWS, glu, 0)

        def body(i, _, cg=cg, cs=cs):
            r0 = pl.multiple_of(i * ROWS, ROWS)
            acc = jnp.zeros((ROWS, gc), f32) + b_ref[:, cs]
            for k in range(taps):
                acc = acc + xp_ref[cg, pl.ds(r0 + (CONV_PAD - (taps - 1) + k), ROWS), :] * w_ref[k:k + 1, cs]
            mu = jnp.mean(acc, axis=-1, keepdims=True)
            dv = acc - mu
            var = jnp.mean(dv * dv, axis=-1, keepdims=True)
            y = dv * lax.rsqrt(var + EPS) * lg_ref[:, cs] + lb_ref[:, cs]
            o_ref[pl.ds(r0, ROWS), cs] = _silu(y).astype(o_ref.dtype)
            return 0

        lax.fori_loop(0, t // ROWS, body, 0, unroll=4)


def _conv_module(z, conv_w, conv_b, ln_g, ln_b):
    b, t, _ = z.shape
    taps, c = conv_w.shape
    assert taps - 1 <= CONV_PAD and c // CONV_GROUPS == LANE
    return pl.pallas_call(
        _conv_kernel,
        grid=(b,),
        in_specs=[
            pl.BlockSpec((None, t, 2 * c), lambda bi: (bi, 0, 0)),
            pl.BlockSpec((taps, c), lambda bi: (0, 0)),
            pl.BlockSpec((1, c), lambda bi: (0, 0)),
            pl.BlockSpec((1, c), lambda bi: (0, 0)),
            pl.BlockSpec((1, c), lambda bi: (0, 0)),
        ],
        out_specs=pl.BlockSpec((None, t, c), lambda bi: (bi, 0, 0)),
        out_shape=jax.ShapeDtypeStruct((b, t, c), bf16),
        scratch_shapes=[pltpu.VMEM((CONV_GROUPS, t + CONV_PAD, c // CONV_GROUPS), f32)],
        compiler_params=_cparams("parallel"),
        name="conv_module",
    )(z, conv_w, conv_b.reshape(1, c), ln_g.reshape(1, c), ln_b.reshape(1, c))


ML_PAD = 8


def _mlstm_kernel(q_ref, k_ref, v_ref, o_ref, gr_ref, cw_ref, cb_ref, bic_ref, bfc_ref,
                  ng_ref, y_ref, qp_ref, kp_ref, qs_ref, ks_ref, va_ref, cta_ref, m_ref):
    t, dm = q_ref.shape
    nh = dm // HEAD_DIM
    L = ML_CHUNK
    nc = t // L
    taps = cw_ref.shape[0]
    kscale = HEAD_DIM ** -0.5

    for h in range(nh):
        hs = slice(h * HEAD_DIM, (h + 1) * HEAD_DIM)
        ks_cols = slice(dm + h * HEAD_DIM, dm + (h + 1) * HEAD_DIM)
        qp_ref[h, 0:ML_PAD, :] = jnp.zeros((ML_PAD, HEAD_DIM), f32)
        kp_ref[h, 0:ML_PAD, :] = jnp.zeros((ML_PAD, HEAD_DIM), f32)

        def stage(i, _, h=h, hs=hs):
            r0 = pl.multiple_of(i * ROWS, ROWS)
            qp_ref[h, pl.ds(ML_PAD + r0, ROWS), :] = q_ref[pl.ds(r0, ROWS), hs].astype(f32)
            kp_ref[h, pl.ds(ML_PAD + r0, ROWS), :] = k_ref[pl.ds(r0, ROWS), hs].astype(f32)
            return 0

        lax.fori_loop(0, t // ROWS, stage, 0)

        def sconv(i, _, h=h, hs=hs, ks_cols=ks_cols):
            r0 = pl.multiple_of(i * ROWS, ROWS)
            aq = jnp.zeros((ROWS, HEAD_DIM), f32) + cb_ref[:, hs]
            ak = jnp.zeros((ROWS, HEAD_DIM), f32) + cb_ref[:, ks_cols]
            for j in range(taps):
                off = ML_PAD - (taps - 1) + j
                aq = aq + qp_ref[h, pl.ds(r0 + off, ROWS), :] * cw_ref[j:j + 1, hs]
                ak = ak + kp_ref[h, pl.ds(r0 + off, ROWS), :] * cw_ref[j:j + 1, ks_cols]
            qs_ref[pl.ds(r0, ROWS), hs] = _silu(aq).astype(bf16)
            ks_ref[pl.ds(r0, ROWS), hs] = (_silu(ak) * kscale).astype(bf16)
            return 0

        lax.fori_loop(0, t // ROWS, sconv, 0)

    for h in range(nh):
        hs = slice(h * HEAD_DIM, (h + 1) * HEAD_DIM)
        va_ref[:, 2 * h * HEAD_DIM:(2 * h + 1) * HEAD_DIM] = v_ref[:, hs]
        va_ref[:, (2 * h + 1) * HEAD_DIM:(2 * h + 2) * HEAD_DIM] = jnp.ones((t, HEAD_DIM), bf16)

    ri = lax.broadcasted_iota(jnp.int32, (L, L), 0)
    ci = lax.broadcasted_iota(jnp.int32, (L, L), 1)
    causal = ri >= ci
    lower_f = causal.astype(f32)
    eye_f = (ri == ci).astype(f32)
    eye_b = eye_f.astype(bf16)
    upper_b = (ri <= ci).astype(bf16)
    ones_b = jnp.ones((L, HEAD_DIM), bf16)
    lane = lax.broadcasted_iota(jnp.int32, (nh, L), 1)

    def log_sigmoid(v):
        return jnp.minimum(v, 0.0) - jnp.log(1.0 + jnp.exp(-jnp.abs(v)))

    def split(a, pieces):
        out = []
        for _ in range(pieces - 1):
            p = a.astype(bf16)
            out.append(p)
            a = a - p.astype(f32)
        out.append(a.astype(bf16))
        return out

    def rows_to_columns(weighted):
        return sum(jnp.dot(p, ones_b, preferred_element_type=f32) for p in split(weighted, 2))

    cta_ref[...] = jnp.zeros_like(cta_ref)
    m_ref[...] = jnp.zeros_like(m_ref)

    def chunk(i, _):
        r0 = pl.multiple_of(i * L, L)
        li = gr_ref[0:nh, pl.ds(r0, L)] + bic_ref[...]
        lf = log_sigmoid(gr_ref[nh:2 * nh, pl.ds(r0, L)] + bfc_ref[...])
        b = sum(jnp.dot(p, upper_b, preferred_element_type=f32) for p in split(lf, 3))
        g = li - b
        cmax = g
        for sh in (1, 2, 4, 8, 16, 32, 64):
            cmax = jnp.maximum(cmax, jnp.where(lane >= sh, pltpu.roll(cmax, sh, axis=1), NEG))
        m_prev = m_ref[...]
        big_m = jnp.maximum(cmax, m_prev)
        m_last = big_m[:, L - 1:L]
        b_last = b[:, L - 1:L]
        wk = jnp.exp(g - m_last)
        decay = jnp.exp(m_prev[:, 0:1] - m_last)
        m_ref[...] = jnp.broadcast_to(b_last + m_last, (nh, L))

        heads = range(nh)
        hsl = [slice(h * HEAD_DIM, (h + 1) * HEAD_DIM) for h in heads]
        q = [qs_ref[pl.ds(r0, L), hsl[h]] for h in heads]
        k = [ks_ref[pl.ds(r0, L), hsl[h]] for h in heads]
        va = [va_ref[pl.ds(r0, L), 2 * h * HEAD_DIM:(2 * h + 2) * HEAD_DIM] for h in heads]
        cta = [cta_ref[h] for h in heads]
        m_col = [rows_to_columns(eye_f * big_m[h:h + 1, :]) for h in heads]
        b_col = [rows_to_columns(lower_f * lf[h:h + 1, :]) for h in heads]
        qk = [_dot_nt(q[h], k[h]) for h in heads]
        k_t = [_dot_nt(eye_b, k[h]) for h in heads]
        inter = [jnp.dot(q[h], cta[h].astype(bf16), preferred_element_type=f32) for h in heads]
        s = [(qk[h] * jnp.exp(jnp.where(causal, g[h:h + 1, :] - m_col[h], NEG))).astype(bf16) for h in heads]
        kw_t = [(k_t[h] * wk[h:h + 1, :]).astype(bf16) for h in heads]
        w_inter = [jnp.exp(m_prev[h:h + 1, 0:1] - m_col[h]) for h in heads]
        both = [jnp.dot(s[h], va[h], preferred_element_type=f32)
                + jnp.concatenate([w_inter[h], w_inter[h]], axis=1) * inter[h] for h in heads]
        new_cta = [decay[h:h + 1, :] * cta[h] + jnp.dot(kw_t[h], va[h], preferred_element_type=f32) for h in heads]
        hout = [both[h][:, 0:HEAD_DIM] / jnp.maximum(jnp.abs(both[h][:, HEAD_DIM:2 * HEAD_DIM]),
                                                     jnp.exp(-(b_col[h] + m_col[h]))) for h in heads]
        ssq = [sum(jnp.dot(pc, ones_b, preferred_element_type=f32) for pc in split(hout[h] * hout[h], 2)) for h in heads]
        for h in heads:
            hn = hout[h] * lax.rsqrt(ssq[h] * (1.0 / HEAD_DIM) + EPS) * ng_ref[:, hsl[h]]
            og = _sigmoid(o_ref[pl.ds(r0, L), hsl[h]].astype(f32))
            y_ref[pl.ds(r0, L), hsl[h]] = (og * hn).astype(y_ref.dtype)
            cta_ref[h] = new_cta[h]
        return 0

    lax.fori_loop(0, nc, chunk, 0, unroll=2)


def _mlstm(z, zg_t, col0, ml_conv_w, ml_conv_b, b_i, b_f, norm_g):
    b, t, _ = z.shape
    dm = norm_g.shape[0]
    nh = dm // HEAD_DIM
    taps = ml_conv_w.shape[0]
    assert taps - 1 <= ML_PAD

    def zspec(j):
        return pl.BlockSpec((None, t, dm), lambda bi: (bi, 0, col0 + j))

    return pl.pallas_call(
        _mlstm_kernel,
        grid=(b,),
        in_specs=[
            zspec(0), zspec(1), zspec(2), zspec(3),
            pl.BlockSpec((None, 2 * nh, t), lambda bi: (bi, 0, 0)),
            pl.BlockSpec((taps, 2 * dm), lambda bi: (0, 0)),
            pl.BlockSpec((1, 2 * dm), lambda bi: (0, 0)),
            pl.BlockSpec((nh, 1), lambda bi: (0, 0)),
            pl.BlockSpec((nh, 1), lambda bi: (0, 0)),
            pl.BlockSpec((1, dm), lambda bi: (0, 0)),
        ],
        out_specs=pl.BlockSpec((None, t, dm), lambda bi: (bi, 0, 0)),
        out_shape=jax.ShapeDtypeStruct((b, t, dm), bf16),
        scratch_shapes=[
            pltpu.VMEM((nh, t + ML_PAD, HEAD_DIM), f32),
            pltpu.VMEM((nh, t + ML_PAD, HEAD_DIM), f32),
            pltpu.VMEM((t, dm), bf16),
            pltpu.VMEM((t, dm), bf16),
            pltpu.VMEM((t, 2 * dm), bf16),
            pltpu.VMEM((nh, HEAD_DIM, 2 * HEAD_DIM), f32),
            pltpu.VMEM((nh, ML_CHUNK), f32),
        ],
        compiler_params=_cparams("parallel"),
        name="mlstm",
    )(z, z, z, z, zg_t, ml_conv_w, ml_conv_b.reshape(1, 2 * dm), b_i.reshape(nh, 1), b_f.reshape(nh, 1),
      norm_g.reshape(1, dm))


MASK_BIG = 1e30
BIAS_COLS = 128


def _moba_kernel(q_ref, k_ref, v_ref, kx_ref, sl_ref, ng_ref, y_ref, km_ref, s_ref, kc_ref, vc_ref):
    t, dh = q_ref.shape
    blk = MOBA_BLOCK
    nb = t // blk

    kc_ref[:, 0:dh] = k_ref[...]
    kc_ref[:, dh:dh + BIAS_COLS] = kx_ref[...]
    vc_ref[:, 0:dh] = v_ref[...]
    vc_ref[:, dh:2 * dh] = jnp.ones((t, dh), bf16)

    for n in range(nb):
        km_ref[n:n + 1, :] = jnp.mean(k_ref[n * blk:(n + 1) * blk, :].astype(f32), axis=0, keepdims=True)
    gate_t = _dot_nt(km_ref[...], q_ref[...].astype(f32), precision=HIGHEST)

    blk_id = lax.broadcasted_iota(jnp.int32, (nb, blk), 0)
    ri = lax.broadcasted_iota(jnp.int32, (blk, blk), 0)
    ci = lax.broadcasted_iota(jnp.int32, (blk, blk), 1)
    causal = ri >= ci
    eye = (ri == ci).astype(bf16)
    lane = lax.broadcasted_iota(jnp.int32, (blk, BIAS_COLS), 1)
    rowf = lax.broadcasted_iota(jnp.int32, (blk, BIAS_COLS), 0).astype(f32)
    slope = sl_ref[:, 0:1]

    def logits_pass(j):
        qj = q_ref[j * blk:(j + 1) * blk, :]
        masked_t = jnp.zeros((nb, blk), f32)
        if j > 0:
            g = jnp.where(blk_id < j, gate_t[:, j * blk:(j + 1) * blk], NEG)
            for n in range(j):
                gn = g[n:n + 1, :]
                ahead = (g > gn) | ((g == gn) & (blk_id < n))
                rank = jnp.sum(jnp.where(ahead, 1.0, 0.0), axis=0, keepdims=True)
                masked_t = jnp.where((blk_id == n) & (rank >= float(MOBA_TOPK)), -1.0, masked_t)
        masked_pad = jnp.concatenate([masked_t, jnp.zeros((BIAS_COLS - nb, blk), f32)], axis=0).astype(bf16)
        qx = _dot_nt(eye, masked_pad)
        qx = jnp.where(lane == nb, -slope * float(blk * j), qx)
        qx = jnp.where(lane == nb + 1, -slope * rowf, qx)
        qx = jnp.where((lane == nb + 2) | (lane == nb + 3), 1.0, qx).astype(bf16)
        qc = jnp.concatenate([qj, qx], axis=1)
        mx = None
        for n in range(j + 1):
            ks = slice(n * blk, (n + 1) * blk)
            lg = _dot_nt(qc, kc_ref[ks, :])
            if n == j:
                lg = jnp.where(causal, lg, NEG)
            s_ref[j % 2, :, ks] = lg
            half = jnp.maximum(lg[:, 0:LANE], lg[:, LANE:2 * LANE])
            mx = half if mx is None else jnp.maximum(mx, half)
        return jnp.max(mx, axis=-1, keepdims=True)

    def value_pass(j, m):
        acc = jnp.zeros((blk, 2 * dh), f32)
        for n in range(j + 1):
            ks = slice(n * blk, (n + 1) * blk)
            p = jnp.exp((s_ref[j % 2, :, ks] - m).astype(bf16))
            acc = acc + jnp.dot(p, vc_ref[ks, :], preferred_element_type=f32)
        o = acc[:, 0:dh] / acc[:, dh:2 * dh]
        o = o * lax.rsqrt(jnp.mean(o * o, axis=-1, keepdims=True) + EPS) * ng_ref[...]
        y_ref[j * blk:(j + 1) * blk, :] = o.astype(y_ref.dtype)

    m_prev = logits_pass(0)
    for j in range(1, nb):
        m_cur = logits_pass(j)
        value_pass(j - 1, m_prev)
        m_prev = m_cur
    value_pass(nb - 1, m_prev)


def _moba(z, col0, norm_g):
    b, t, _ = z.shape
    dh = HEAD_DIM
    nh = norm_g.shape[0] // dh
    blk = MOBA_BLOCK
    assert t % blk == 0
    nb = t // blk
    assert nb + 4 <= BIAS_COLS
    slopes = 2.0 ** (-(8.0 / nh) * jnp.arange(1, nh + 1, dtype=f32))
    pos = jnp.arange(t, dtype=jnp.int32)
    col = jnp.arange(BIAS_COLS, dtype=jnp.int32)[None, :]
    kblk = (pos // blk)[:, None]
    base = jnp.where(col == kblk, MASK_BIG, 0.0) + jnp.where((col == nb) | (col == nb + 1), 1.0, 0.0)
    per_head = (jnp.where(col == nb + 2, (kblk * blk).astype(f32), 0.0)
                + jnp.where(col == nb + 3, (pos % blk).astype(f32)[:, None], 0.0))
    kx = (base[None] + slopes[:, None, None] * per_head[None]).astype(bf16)
    slope_rows = jnp.broadcast_to(slopes[:, None, None], (nh, 1, LANE))

    def zspec(j):
        return pl.BlockSpec((None, t, dh), lambda bi, h: (bi, 0, col0 + j * nh + h))

    return pl.pallas_call(
        _moba_kernel,
        grid=(b, nh),
        in_specs=[
            zspec(0), zspec(1), zspec(2),
            pl.BlockSpec((None, t, BIAS_COLS), lambda bi, h: (h, 0, 0)),
            pl.BlockSpec((None, 1, LANE), lambda bi, h: (h, 0, 0)),
            pl.BlockSpec((None, 1, dh), lambda bi, h: (h, 0, 0)),
        ],
        out_specs=pl.BlockSpec((None, t, dh), lambda bi, h: (bi, 0, h)),
        out_shape=jax.ShapeDtypeStruct((b, t, nh * dh), bf16),
        scratch_shapes=[pltpu.VMEM((nb, dh), f32), pltpu.VMEM((2, blk, t), f32), pltpu.VMEM((t, dh + BIAS_COLS), bf16),
                        pltpu.VMEM((t, 2 * dh), bf16)],
        compiler_params=_cparams("parallel", "parallel"),
        name="moba_attention",
    )(z, z, z, kx, slope_rows, norm_g.reshape(nh, 1, dh))


def _router_logits_t(w_t32, h32, hb):
    h_lo = (h32 - hb.astype(f32)).astype(bf16)
    w_hi = w_t32.astype(bf16)
    w_lo = (w_t32 - w_hi.astype(f32)).astype(bf16)
    return _dot_nt(w_hi, hb) + _dot_nt(w_hi, h_lo) + _dot_nt(w_lo, hb)


def _top2_t(logits):
    ne = logits.shape[0]
    eid = lax.broadcasted_iota(jnp.int32, logits.shape, 0)
    m1 = jnp.max(logits, axis=0, keepdims=True)
    i1 = jnp.min(jnp.where(logits == m1, eid, ne), axis=0, keepdims=True)
    rest = jnp.where(eid == i1, -jnp.inf, logits)
    m2 = jnp.max(rest, axis=0, keepdims=True)
    i2 = jnp.min(jnp.where(rest == m2, eid, ne), axis=0, keepdims=True)
    e2 = jnp.exp(m2 - m1)
    p1 = 1.0 / (1.0 + e2)
    return i1, i2, p1, e2 * p1


OUTPROJ_SPLIT = 2


def _outproj_kernel(yc_ref, ym_ref, ya_ref, w_ref, x_ref, gt_ref, g_ref, sh_ref, sc_ref, *rest, routed):
    if routed:
        wr_ref, br_ref, xo_ref, h_ref, idx_ref, p_ref = rest
    else:
        xo_ref, h_ref = rest
    tm = x_ref.shape[0]
    sub = tm // OUTPROJ_SPLIT
    blocks = [slice(r * sub, (r + 1) * sub) for r in range(OUTPROJ_SPLIT)]
    accs = [jnp.dot(jnp.concatenate([yc_ref[rs, :], ym_ref[rs, :], ya_ref[rs, :]], axis=1), w_ref[...],
                    preferred_element_type=f32) for rs in blocks]
    for rs, acc in zip(blocks, accs):
        xn = x_ref[rs, :] + gt_ref[...] * acc
        xo_ref[rs, :] = xn
        h = _norm_modulate(xn, g_ref[...], sh_ref[...], sc_ref[...])
        hb = h.astype(bf16)
        h_ref[rs, :] = h if routed else hb
        if routed:
            i1, i2, p1, p2 = _top2_t(_router_logits_t(wr_ref[...], h, hb) + br_ref[...])
            first = lax.broadcasted_iota(jnp.int32, (TOP_K, sub), 0) == 0
            idx_ref[:, rs] = jnp.where(first, i1, i2)
            p_ref[:, rs] = jnp.where(first, p1, p2)


def _out_projection(y_conv, y_ml, y_att, w_out, x, mod4, layer, g_ffn, router=None, tm=512):
    b, t, d = x.shape
    c1, c2, c3 = y_conv.shape[-1], y_ml.shape[-1], y_att.shape[-1]
    assert c1 + c2 + c3 == w_out.shape[0]
    row = lambda c: pl.BlockSpec((None, tm, c), lambda bi, i: (bi, i, 0))
    modspec = lambda chunk: pl.BlockSpec((None, None, 1, d), lambda bi, i: (layer, bi, 0, chunk))
    in_specs = [
        row(c1), row(c2), row(c3),
        pl.BlockSpec(w_out.shape, lambda bi, i: (0, 0)),
        row(d), modspec(2),
        pl.BlockSpec((1, d), lambda bi, i: (0, 0)), modspec(3), modspec(4),
    ]
    args = [y_conv, y_ml, y_att, w_out, x, mod4, g_ffn.reshape(1, d), mod4, mod4]
    out_specs = [row(d), row(d)]
    out_shape = [jax.ShapeDtypeStruct((b, t, d), f32), jax.ShapeDtypeStruct((b, t, d), bf16 if router is None else f32)]
    if router is not None:
        w_r, b_r = router
        ne = w_r.shape[1]
        in_specs += [pl.BlockSpec((ne, d), lambda bi, i: (0, 0)), pl.BlockSpec((ne, 1), lambda bi, i: (0, 0))]
        args += [w_r.T, b_r.reshape(ne, 1)]
        kspec = pl.BlockSpec((None, TOP_K, tm), lambda bi, i: (bi, 0, i))
        out_specs += [kspec, kspec]
        out_shape += [jax.ShapeDtypeStruct((b, TOP_K, t), jnp.int32), jax.ShapeDtypeStruct((b, TOP_K, t), f32)]
    return pl.pallas_call(
        functools.partial(_outproj_kernel, routed=router is not None),
        grid=(b, t // tm),
        in_specs=in_specs,
        out_specs=out_specs,
        out_shape=out_shape,
        compiler_params=_cparams("parallel", "parallel"),
        name="out_projection",
    )(*args)


def _swiglu_rows(hb, wg_ref, wu_ref, wd_ref):
    g = jnp.dot(hb, wg_ref[...].astype(bf16), preferred_element_type=f32)
    u = jnp.dot(hb, wu_ref[...].astype(bf16), preferred_element_type=f32)
    a = (_silu(g) * u).astype(bf16)
    return jnp.dot(a, wd_ref[...].astype(bf16), preferred_element_type=f32)


def _dense_swiglu_kernel(h_ref, wg_ref, wu_ref, wd_ref, o_ref, acc_ref):
    f = pl.program_id(1)

    @pl.when(f == 0)
    def _():
        acc_ref[...] = jnp.zeros_like(acc_ref)

    acc_ref[...] += _swiglu_rows(h_ref[...], wg_ref, wu_ref, wd_ref)

    @pl.when(f == pl.num_programs(1) - 1)
    def _():
        o_ref[...] = acc_ref[...].astype(o_ref.dtype)


def _dense_swiglu(h, wg, wu, wd, layer, tm=1024, tf=512):
    n, d = h.shape
    ff = wg.shape[-1]
    return pl.pallas_call(
        _dense_swiglu_kernel,
        grid=(n // tm, ff // tf),
        in_specs=[
            pl.BlockSpec((tm, d), lambda i, f: (i, 0)),
            pl.BlockSpec((None, d, tf), lambda i, f: (layer, 0, f)),
            pl.BlockSpec((None, d, tf), lambda i, f: (layer, 0, f)),
            pl.BlockSpec((None, tf, d), lambda i, f: (layer, f, 0)),
        ],
        out_specs=pl.BlockSpec((tm, d), lambda i, f: (i, 0)),
        out_shape=jax.ShapeDtypeStruct((n, d), bf16),
        scratch_shapes=[pltpu.VMEM((tm, d), f32)],
        compiler_params=_cparams("parallel", "arbitrary"),
        name="dense_swiglu",
    )(h, wg, wu, wd)


QUARTER = 256
GATHER_UNROLL = 8


def _moe_swiglu_kernel(te_ref, nq_ref, tok_ref, tok_next_ref, h_hbm, wg_ref, wu_ref, wd_ref, o_ref,
                       hrows_ref, hb_ref, acc_ref, sem):
    i = pl.program_id(0)
    f = pl.program_id(1)
    nt = pl.num_programs(0)
    nq = nq_ref[i]
    tm = acc_ref.shape[0]

    def row_copy(src_row, dst_row, rows=1):
        return pltpu.make_async_copy(h_hbm.at[pl.ds(src_row, rows)], hrows_ref.at[pl.ds(dst_row, rows)], sem)

    def start_gather(tokens_ref, quarters):
        def issue(g, _):
            for u in range(GATHER_UNROLL):
                r = g * GATHER_UNROLL + u
                row_copy(tokens_ref[0, r], r).start()
            return 0

        lax.fori_loop(0, quarters * (QUARTER // GATHER_UNROLL), issue, 0)

    @pl.when(f == 0)
    def _():
        acc_ref[...] = jnp.zeros_like(acc_ref)

        @pl.when((i == 0) & (nq > 0))
        def _():
            start_gather(tok_ref, nq)

        @pl.when(nq > 0)
        def _():
            row_copy(0, 0, nq * QUARTER).wait()
            for q in range(tm // QUARTER):
                @pl.when(q < nq)
                def _(q=q):
                    qs = slice(q * QUARTER, (q + 1) * QUARTER)
                    hb_ref[qs, :] = hrows_ref[qs, :].astype(bf16)

        nq_next = nq_ref[jnp.minimum(i + 1, nt - 1)]

        @pl.when((i + 1 < nt) & (nq_next > 0))
        def _():
            start_gather(tok_next_ref, nq_next)

    for q in range(1, tm // QUARTER + 1):
        @pl.when(nq == q)
        def _(q=q):
            rows = q * QUARTER
            acc_ref[0:rows, :] += _swiglu_rows(hb_ref[0:rows, :], wg_ref, wu_ref, wd_ref)

    @pl.when(f == pl.num_programs(1) - 1)
    def _():
        o_ref[...] = acc_ref[...].astype(o_ref.dtype)


def _moe_swiglu(h32, row_token, tile_expert, tile_quarters, wg, wu, wd, tm=1024, tf=256):
    n, d = h32.shape
    nt = row_token.shape[0]
    ff = wg.shape[-1]
    nf = ff // tf
    assert tm % QUARTER == 0

    def wmap(i, f, te, nq):
        return (te[i], 0, jnp.where(nq[i] > 0, f, nf - 1))

    def wdmap(i, f, te, nq):
        return (te[i], jnp.where(nq[i] > 0, f, nf - 1), 0)

    grid_spec = pltpu.PrefetchScalarGridSpec(
        num_scalar_prefetch=2,
        grid=(nt, nf),
        in_specs=[
            pl.BlockSpec((None, 1, tm), lambda i, f, te, nq: (i, 0, 0), memory_space=pltpu.SMEM),
            pl.BlockSpec((None, 1, tm), lambda i, f, te, nq: (jnp.minimum(i + 1, nt - 1), 0, 0),
                         memory_space=pltpu.SMEM),
            pl.BlockSpec(memory_space=pl.ANY),
            pl.BlockSpec((None, d, tf), wmap),
            pl.BlockSpec((None, d, tf), wmap),
            pl.BlockSpec((None, tf, d), wdmap),
        ],
        out_specs=pl.BlockSpec((tm, d), lambda i, f, te, nq: (i, 0)),
        scratch_shapes=[
            pltpu.VMEM((tm, d), f32),
            pltpu.VMEM((tm, d), bf16),
            pltpu.VMEM((tm, d), f32),
            pltpu.SemaphoreType.DMA,
        ],
    )
    return pl.pallas_call(
        _moe_swiglu_kernel,
        grid_spec=grid_spec,
        out_shape=jax.ShapeDtypeStruct((nt * tm, d), bf16),
        compiler_params=_cparams("arbitrary", "arbitrary"),
        name="moe_swiglu",
    )(tile_expert, tile_quarters, row_token, row_token, h32, wg, wu, wd)


def _combine_kernel(*refs, n_y, weighted, last):
    x_ref, gt_ref = refs[0], refs[1]
    ys = [r[...].astype(f32) for r in refs[2:2 + n_y]]
    rest = refs[2 + n_y:]
    if weighted:
        p = rest[0][...]
        rest = rest[1:]
        ys = [p[:, k:k + 1] * y for k, y in enumerate(ys)]
    y = ys[0]
    for yk in ys[1:]:
        y = y + yk
    xn = x_ref[...] + gt_ref[...] * y
    if last:
        g_ref, o_ref = rest
        o_ref[...] = xn * lax.rsqrt(jnp.mean(xn * xn, axis=-1, keepdims=True) + EPS) * g_ref[...]
    else:
        g_ref, sh_ref, sc_ref, xo_ref, h_ref = rest
        xo_ref[...] = xn
        h_ref[...] = _norm_modulate(xn, g_ref[...], sh_ref[...], sc_ref[...]).astype(h_ref.dtype)


def _combine(x, mod4, layer, ys, probs, g_next, last, tm=512):
    b, t, d = x.shape
    spec = pl.BlockSpec((None, tm, d), lambda bi, i: (bi, i, 0))
    modspec = lambda lyr, chunk: pl.BlockSpec((None, None, 1, d), lambda bi, i: (lyr, bi, 0, chunk))
    in_specs = [spec, modspec(layer, 5)] + [spec] * len(ys)
    args = [x, mod4, *ys]
    if probs is not None:
        in_specs.append(pl.BlockSpec((None, tm, len(ys)), lambda bi, i: (bi, i, 0)))
        args.append(probs)
    in_specs.append(pl.BlockSpec((1, d), lambda bi, i: (0, 0)))
    args.append(g_next.reshape(1, d))
    if last:
        out_specs, out_shape = spec, jax.ShapeDtypeStruct((b, t, d), f32)
    else:
        in_specs += [modspec(layer + 1, 0), modspec(layer + 1, 1)]
        args += [mod4, mod4]
        out_specs = [spec, spec]
        out_shape = [jax.ShapeDtypeStruct((b, t, d), f32), jax.ShapeDtypeStruct((b, t, d), bf16)]
    return pl.pallas_call(
        functools.partial(_combine_kernel, n_y=len(ys), weighted=probs is not None, last=last),
        grid=(b, t // tm),
        in_specs=in_specs,
        out_specs=out_specs,
        out_shape=out_shape,
        compiler_params=_cparams("parallel", "parallel"),
        name="ffn_combine",
    )(*args)


def _routing_tables(idx, n_experts, tm):
    n = idx.shape[0]
    na = n * TOP_K
    nt = na // tm + n_experts
    e_flat = idx.reshape(na)
    onehot = (e_flat[:, None] == jnp.arange(n_experts, dtype=jnp.int32)[None, :]).astype(jnp.int32)
    counts = jnp.sum(onehot, axis=0)
    rank = jnp.sum((jnp.cumsum(onehot, axis=0) - 1) * onehot, axis=1)
    tiles_per = (counts + tm - 1) // tm
    tile_end = jnp.cumsum(tiles_per)
    tile_start = tile_end - tiles_per
    slot = tile_start[e_flat] * tm + rank
    n_valid = tile_end[-1:].astype(jnp.int32)
    tile_ids = jnp.arange(nt, dtype=jnp.int32)
    tile_expert = jnp.sum((tile_ids[:, None] >= tile_end[None, :]).astype(jnp.int32), axis=1)
    last_expert = jnp.sum((n_valid - 1 >= tile_end).astype(jnp.int32))
    tile_expert = jnp.where(tile_ids < n_valid, tile_expert, last_expert).astype(jnp.int32)
    tile_rows = jnp.clip(counts[tile_expert] - (tile_ids - tile_start[tile_expert]) * tm, 0, tm)
    tile_quarters = jnp.where(tile_ids < n_valid, (tile_rows + QUARTER - 1) // QUARTER, 0).astype(jnp.int32)
    token = jnp.arange(na, dtype=jnp.int32) // TOP_K
    row_token = jnp.zeros((nt * tm,), jnp.int32).at[slot].set(token, unique_indices=True, mode="promise_in_bounds")
    return slot.reshape(n, TOP_K), row_token.reshape(nt, 1, tm), tile_expert, tile_quarters


def kernel(x, c, w_mod, b_mod, g_mix, g_ffn, w_in, conv_w, conv_b, conv_ln_g, conv_ln_b, ml_conv_w, ml_conv_b, ml_b_i, ml_b_f, ml_norm_g, attn_norm_g, w_out, ffn_w_gate, ffn_w_up, ffn_w_down, moe_w_router, moe_b_router, moe_w_gate, moe_w_up, moe_w_down, g_final):
    b, t, d = x.shape
    depth = w_mod.shape[0]
    conv_ch = conv_w.shape[-1]
    ml_dim = ml_norm_g.shape[-1]
    ml_heads = ml_b_i.shape[-1]
    n_experts = moe_w_router.shape[-1]
    att_dim = attn_norm_g.shape[-1]
    n_tok = b * t
    tm_ffn = 1024

    mod = _adaln_mod(c, w_mod, b_mod)
    mod4 = mod.reshape(depth, b, 1, 6 * d)

    gate0 = 2 * conv_ch + 4 * ml_dim
    gate1 = gate0 + 2 * ml_heads
    w_in_t = jnp.swapaxes(w_in, 1, 2)

    h = _prep(x, g_mix[0], mod4, 0, 0)
    out = None
    for l in range(depth):
        z, zg = _in_projection(h.reshape(n_tok, d), w_in_t, l, gate0, gate1, att_dim, HEAD_DIM ** -0.5)
        z = z.reshape(b, t, -1)
        zg = zg.reshape(b, t, -1)
        zg_t = jnp.swapaxes(zg, 1, 2)

        y_conv = _conv_module(z, conv_w[l], conv_b[l], conv_ln_g[l], conv_ln_b[l])
        y_ml = _mlstm(z, zg_t, (2 * conv_ch) // ml_dim, ml_conv_w[l], ml_conv_b[l], ml_b_i[l], ml_b_f[l], ml_norm_g[l])
        y_att = _moba(z, gate0 // HEAD_DIM, attn_norm_g[l])

        j = l // 2
        w_o = w_out[l].astype(bf16)
        if l % 2 == 0:
            x, h = _out_projection(y_conv, y_ml, y_att, w_o, x, mod4, l, g_ffn[l])
            y = _dense_swiglu(h.reshape(n_tok, d), ffn_w_gate, ffn_w_up, ffn_w_down, j, tm=tm_ffn)
            ys, y_probs = [y.reshape(b, t, d)], None
        else:
            x, h32, idx, probs = _out_projection(y_conv, y_ml, y_att, w_o, x, mod4, l, g_ffn[l],
                                                 router=(moe_w_router[j], moe_b_router[j]))
            slot, row_token, tile_expert, tile_quarters = _routing_tables(
                jnp.swapaxes(idx, 1, 2).reshape(n_tok, TOP_K), n_experts, tm_ffn)
            ff = moe_w_gate.shape[-1]
            y_sorted = _moe_swiglu(
                h32.reshape(n_tok, d), row_token, tile_expert + j * n_experts, tile_quarters,
                moe_w_gate.reshape(-1, d, ff), moe_w_up.reshape(-1, d, ff), moe_w_down.reshape(-1, ff, d), tm=tm_ffn)
            ys = [y_sorted.at[slot[:, k]].get(mode="promise_in_bounds").reshape(b, t, d) for k in range(TOP_K)]
            y_probs = jnp.swapaxes(probs, 1, 2)

        if l + 1 < depth:
            x, h = _combine(x, mod4, l, ys, y_probs, g_mix[l + 1], last=False)
        else:
            out = _combine(x, mod4, l, ys, y_probs, g_final, last=True)
    return out
```

```python
import functools

import jax
import jax.numpy as jnp
from jax import lax
from jax.experimental import pallas as pl
from jax.experimental.pallas import tpu as pltpu

f32 = jnp.float32
bf16 = jnp.bfloat16
HIGHEST = lax.Precision.HIGHEST

CONV_GROUPS = 4
HEAD_DIM = 128
ML_CHUNK = 128
MOBA_BLOCK = 256
MOBA_TOPK = 3
TOP_K = 2
EPS = 1e-6
NEG = -1e30

VMEM_LIMIT_BYTES = 56 * 1024 * 1024
LANE = 128


def _cparams(*sem):
    return pltpu.CompilerParams(dimension_semantics=sem, vmem_limit_bytes=VMEM_LIMIT_BYTES)


def _sigmoid(v):
    return 1.0 / (1.0 + jnp.exp(-v))


def _silu(v):
    return v * _sigmoid(v)


def _dot_nt(a, b, **kw):
    return lax.dot_general(a, b, (((1,), (1,)), ((), ())), preferred_element_type=f32, **kw)


def _dot_tn(a, b, **kw):
    return lax.dot_general(a, b, (((0,), (0,)), ((), ())), preferred_element_type=f32, **kw)


def _mod_kernel(c_ref, w_ref, b_ref, o_ref):
    cond = _silu(c_ref[...]).astype(bf16)
    o_ref[...] = jnp.dot(cond, w_ref[...].astype(bf16), preferred_element_type=f32) + b_ref[...]


def _adaln_mod(c, w_mod, b_mod, tn=1024):
    depth, d, n6 = w_mod.shape
    b = c.shape[0]
    return pl.pallas_call(
        _mod_kernel,
        grid=(depth, n6 // tn),
        in_specs=[
            pl.BlockSpec((b, d), lambda l, n: (0, 0)),
            pl.BlockSpec((None, d, tn), lambda l, n: (l, 0, n)),
            pl.BlockSpec((None, 1, tn), lambda l, n: (l, 0, n)),
        ],
        out_specs=pl.BlockSpec((None, b, tn), lambda l, n: (l, 0, n)),
        out_shape=jax.ShapeDtypeStruct((depth, b, n6), f32),
        compiler_params=_cparams("parallel", "parallel"),
        name="adaln_mod",
    )(c, w_mod, b_mod.reshape(depth, 1, n6))


def _norm_modulate(x, g, sh, sc):
    ms = jnp.mean(x * x, axis=-1, keepdims=True)
    h = x * lax.rsqrt(ms + EPS) * g
    return h * (1.0 + sc) + sh


def _prep_kernel(x_ref, g_ref, sh_ref, sc_ref, h_ref):
    h_ref[...] = _norm_modulate(x_ref[...], g_ref[...], sh_ref[...], sc_ref[...]).astype(h_ref.dtype)


def _prep(x, g, mod4, layer, chunk, tm=512):
    b, t, d = x.shape
    return pl.pallas_call(
        _prep_kernel,
        grid=(b, t // tm),
        in_specs=[
            pl.BlockSpec((None, tm, d), lambda bi, i: (bi, i, 0)),
            pl.BlockSpec((1, d), lambda bi, i: (0, 0)),
            pl.BlockSpec((None, None, 1, d), lambda bi, i: (layer, bi, 0, chunk)),
            pl.BlockSpec((None, None, 1, d), lambda bi, i: (layer, bi, 0, chunk + 1)),
        ],
        out_specs=pl.BlockSpec((None, tm, d), lambda bi, i: (bi, i, 0)),
        out_shape=jax.ShapeDtypeStruct((b, t, d), bf16),
        compiler_params=_cparams("parallel", "parallel"),
        name="prep",
    )(x, g.reshape(1, d), mod4, mod4)


def _inproj_kernel(h_ref, wa_ref, wb_ref, wg_ref, z_ref, zg_ref, *, n_a, n_q, q_scale):
    j = pl.program_id(1)

    @pl.when(j == 0)
    def _():
        zg_ref[...] = _dot_nt(h_ref[...], wg_ref[...].astype(bf16))

    @pl.when(j < n_a)
    def _():
        z_ref[...] = _dot_nt(h_ref[...], wa_ref[...].astype(bf16)).astype(bf16)

    @pl.when((j >= n_a) & (j < n_a + n_q))
    def _():
        z_ref[...] = (_dot_nt(h_ref[...], wb_ref[0].astype(bf16)) * q_scale).astype(bf16)

    @pl.when(j >= n_a + n_q)
    def _():
        z_ref[...] = _dot_nt(h_ref[...], wb_ref[0].astype(bf16)).astype(bf16)


def _in_projection(h, w_t, layer, gate0, gate1, q_features, q_scale, tm=2048, tn=512):
    n, d = h.shape
    n_in = w_t.shape[1]
    ng = gate1 - gate0
    n_a = gate0 // tn
    n_b = (n_in - gate1) // tn
    n_q = q_features // tn
    assert n_a * tn == gate0 and n_b * tn == n_in - gate1 and n_q * tn == q_features
    assert gate0 % ng == 0 and gate1 % 8 == 0
    return pl.pallas_call(
        functools.partial(_inproj_kernel, n_a=n_a, n_q=n_q, q_scale=q_scale),
        grid=(n // tm, n_a + n_b),
        in_specs=[
            pl.BlockSpec((tm, d), lambda i, j: (i, 0)),
            pl.BlockSpec((None, tn, d), lambda i, j: (layer, jnp.minimum(j, n_a - 1), 0)),
            pl.BlockSpec((pl.Element(1), pl.Element(tn), pl.Element(d)),
                         lambda i, j: (layer, pl.multiple_of(gate1 + tn * jnp.maximum(j - n_a, 0), 8), 0)),
            pl.BlockSpec((None, ng, d), lambda i, j: (layer, gate0 // ng, 0)),
        ],
        out_specs=[
            pl.BlockSpec((tm, tn), lambda i, j: (i, j)),
            pl.BlockSpec((tm, ng), lambda i, j: (i, 0)),
        ],
        out_shape=[
            jax.ShapeDtypeStruct((n, (n_a + n_b) * tn), bf16),
            jax.ShapeDtypeStruct((n, ng), f32),
        ],
        compiler_params=_cparams("parallel", "arbitrary"),
        name="in_projection",
    )(h, w_t, w_t, w_t)


CONV_PAD = 32
ROWS = 128


def _conv_kernel(ag_ref, w_ref, b_ref, lg_ref, lb_ref, o_ref, xp_ref):
    t, c = o_ref.shape
    taps = w_ref.shape[0]
    gc = c // CONV_GROUPS
    for cg in range(CONV_GROUPS):
        cs = slice(cg * gc, (cg + 1) * gc)
        gs = slice(c + cg * gc, c + (cg + 1) * gc)
        xp_ref[cg, 0:CONV_PAD, :] = jnp.zeros((CONV_PAD, gc), f32)

        def glu(i, _, cg=cg, cs=cs, gs=gs):
            r0 = pl.multiple_of(i * ROWS, ROWS)
            a = ag_ref[pl.ds(r0, ROWS), cs].astype(f32)
            g = ag_ref[pl.ds(r0, ROWS), gs].astype(f32)
            xp_ref[cg, pl.ds(CONV_PAD + r0, ROWS), :] = a * _sigmoid(g)
            return 0

        lax.fori_loop(0, t // ROWS, glu, 0)

        def body(i, _, cg=cg, cs=cs):
            r0 = pl.multiple_of(i * ROWS, ROWS)
            acc = jnp.zeros((ROWS, gc), f32) + b_ref[:, cs]
            for k in range(taps):
                acc = acc + xp_ref[cg, pl.ds(r0 + (CONV_PAD - (taps - 1) + k), ROWS), :] * w_ref[k:k + 1, cs]
            mu = jnp.mean(acc, axis=-1, keepdims=True)
            dv = acc - mu
            var = jnp.mean(dv * dv, axis=-1, keepdims=True)
            y = dv * lax.rsqrt(var + EPS) * lg_ref[:, cs] + lb_ref[:, cs]
            o_ref[pl.ds(r0, ROWS), cs] = _silu(y).astype(o_ref.dtype)
            return 0

        lax.fori_loop(0, t // ROWS, body, 0, unroll=4)


def _conv_module(z, conv_w, conv_b, ln_g, ln_b):
    b, t, _ = z.shape
    taps, c = conv_w.shape
    assert taps - 1 <= CONV_PAD and c // CONV_GROUPS == LANE
    return pl.pallas_call(
        _conv_kernel,
        grid=(b,),
        in_specs=[
            pl.BlockSpec((None, t, 2 * c), lambda bi: (bi, 0, 0)),
            pl.BlockSpec((taps, c), lambda bi: (0, 0)),
            pl.BlockSpec((1, c), lambda bi: (0, 0)),
            pl.BlockSpec((1, c), lambda bi: (0, 0)),
            pl.BlockSpec((1, c), lambda bi: (0, 0)),
        ],
        out_specs=pl.BlockSpec((None, t, c), lambda bi: (bi, 0, 0)),
        out_shape=jax.ShapeDtypeStruct((b, t, c), bf16),
        scratch_shapes=[pltpu.VMEM((CONV_GROUPS, t + CONV_PAD, c // CONV_GROUPS), f32)],
        compiler_params=_cparams("parallel"),
        name="conv_module",
    )(z, conv_w, conv_b.reshape(1, c), ln_g.reshape(1, c), ln_b.reshape(1, c))


ML_PAD = 8


def _mlstm_kernel(q_ref, k_ref, v_ref, o_ref, gr_ref, cw_ref, cb_ref, bic_ref, bfc_ref,
                  ng_ref, y_ref, qp_ref, kp_ref, qs_ref, ks_ref, va_ref, cta_ref, m_ref):
    t, dm = q_ref.shape
    nh = dm // HEAD_DIM
    L = ML_CHUNK
    nc = t // L
    taps = cw_ref.shape[0]
    kscale = HEAD_DIM ** -0.5

    for h in range(nh):
        hs = slice(h * HEAD_DIM, (h + 1) * HEAD_DIM)
        ks_cols = slice(dm + h * HEAD_DIM, dm + (h + 1) * HEAD_DIM)
        qp_ref[h, 0:ML_PAD, :] = jnp.zeros((ML_PAD, HEAD_DIM), f32)
        kp_ref[h, 0:ML_PAD, :] = jnp.zeros((ML_PAD, HEAD_DIM), f32)

        def stage(i, _, h=h, hs=hs):
            r0 = pl.multiple_of(i * ROWS, ROWS)
            qp_ref[h, pl.ds(ML_PAD + r0, ROWS), :] = q_ref[pl.ds(r0, ROWS), hs].astype(f32)
            kp_ref[h, pl.ds(ML_PAD + r0, ROWS), :] = k_ref[pl.ds(r0, ROWS), hs].astype(f32)
            return 0

        lax.fori_loop(0, t // ROWS, stage, 0)

        def sconv(i, _, h=h, hs=hs, ks_cols=ks_cols):
            r0 = pl.multiple_of(i * ROWS, ROWS)
            aq = jnp.zeros((ROWS, HEAD_DIM), f32) + cb_ref[:, hs]
            ak = jnp.zeros((ROWS, HEAD_DIM), f32) + cb_ref[:, ks_cols]
            for j in range(taps):
                off = ML_PAD - (taps - 1) + j
                aq = aq + qp_ref[h, pl.ds(r0 + off, ROWS), :] * cw_ref[j:j + 1, hs]
                ak = ak + kp_ref[h, pl.ds(r0 + off, ROWS), :] * cw_ref[j:j + 1, ks_cols]
            qs_ref[pl.ds(r0, ROWS), hs] = _silu(aq).astype(bf16)
            ks_ref[pl.ds(r0, ROWS), hs] = (_silu(ak) * kscale).astype(bf16)
            return 0

        lax.fori_loop(0, t // ROWS, sconv, 0, unroll=2)

    for h in range(nh):
        hs = slice(h * HEAD_DIM, (h + 1) * HEAD_DIM)
        va_ref[:, 2 * h * HEAD_DIM:(2 * h + 1) * HEAD_DIM] = v_ref[:, hs]
        va_ref[:, (2 * h + 1) * HEAD_DIM:(2 * h + 2) * HEAD_DIM] = jnp.ones((t, HEAD_DIM), bf16)

    ri = lax.broadcasted_iota(jnp.int32, (L, L), 0)
    ci = lax.broadcasted_iota(jnp.int32, (L, L), 1)
    causal = ri >= ci
    lower_f = causal.astype(f32)
    eye_f = (ri == ci).astype(f32)
    eye_b = eye_f.astype(bf16)
    upper_b = (ri <= ci).astype(bf16)
    ones_b = jnp.ones((L, HEAD_DIM), bf16)
    lane = lax.broadcasted_iota(jnp.int32, (nh, L), 1)

    def log_sigmoid(v):
        return jnp.minimum(v, 0.0) - jnp.log(1.0 + jnp.exp(-jnp.abs(v)))

    def split(a, pieces):
        out = []
        for _ in range(pieces - 1):
            p = a.astype(bf16)
            out.append(p)
            a = a - p.astype(f32)
        out.append(a.astype(bf16))
        return out

    def rows_to_columns(weighted):
        return sum(jnp.dot(p, ones_b, preferred_element_type=f32) for p in split(weighted, 2))

    cta_ref[...] = jnp.zeros_like(cta_ref)
    m_ref[...] = jnp.zeros_like(m_ref)

    def chunk(i, _):
        r0 = pl.multiple_of(i * L, L)
        li = gr_ref[0:nh, pl.ds(r0, L)] + bic_ref[...]
        lf = log_sigmoid(gr_ref[nh:2 * nh, pl.ds(r0, L)] + bfc_ref[...])
        b = sum(jnp.dot(p, upper_b, preferred_element_type=f32) for p in split(lf, 3))
        g = li - b
        cmax = g
        for sh in (1, 2, 4, 8, 16, 32, 64):
            cmax = jnp.maximum(cmax, jnp.where(lane >= sh, pltpu.roll(cmax, sh, axis=1), NEG))
        m_prev = m_ref[...]
        big_m = jnp.maximum(cmax, m_prev)
        m_last = big_m[:, L - 1:L]
        b_last = b[:, L - 1:L]
        wk = jnp.exp(g - m_last)
        decay = jnp.exp(m_prev[:, 0:1] - m_last)
        m_ref[...] = jnp.broadcast_to(b_last + m_last, (nh, L))

        heads = range(nh)
        hsl = [slice(h * HEAD_DIM, (h + 1) * HEAD_DIM) for h in heads]
        q = [qs_ref[pl.ds(r0, L), hsl[h]] for h in heads]
        k = [ks_ref[pl.ds(r0, L), hsl[h]] for h in heads]
        va = [va_ref[pl.ds(r0, L), 2 * h * HEAD_DIM:(2 * h + 2) * HEAD_DIM] for h in heads]
        cta = [cta_ref[h] for h in heads]
        m_col = [rows_to_columns(eye_f * big_m[h:h + 1, :]) for h in heads]
        b_col = [rows_to_columns(lower_f * lf[h:h + 1, :]) for h in heads]
        qk = [_dot_nt(q[h], k[h]) for h in heads]
        k_t = [_dot_nt(eye_b, k[h]) for h in heads]
        inter = [jnp.dot(q[h], cta[h].astype(bf16), preferred_element_type=f32) for h in heads]
        s = [(qk[h] * jnp.exp(jnp.where(causal, g[h:h + 1, :] - m_col[h], NEG))).astype(bf16) for h in heads]
        kw_t = [(k_t[h] * wk[h:h + 1, :]).astype(bf16) for h in heads]
        w_inter = [jnp.exp(m_prev[h:h + 1, 0:1] - m_col[h]) for h in heads]
        both = [jnp.dot(s[h], va[h], preferred_element_type=f32)
                + jnp.concatenate([w_inter[h], w_inter[h]], axis=1) * inter[h] for h in heads]
        new_cta = [decay[h:h + 1, :] * cta[h] + jnp.dot(kw_t[h], va[h], preferred_element_type=f32) for h in heads]
        hout = [both[h][:, 0:HEAD_DIM] / jnp.maximum(jnp.abs(both[h][:, HEAD_DIM:2 * HEAD_DIM]),
                                                     jnp.exp(-(b_col[h] + m_col[h]))) for h in heads]
        ssq = [sum(jnp.dot(pc, ones_b, preferred_element_type=f32) for pc in split(hout[h] * hout[h], 2)) for h in heads]
        for h in heads:
            hn = hout[h] * lax.rsqrt(ssq[h] * (1.0 / HEAD_DIM) + EPS) * ng_ref[:, hsl[h]]
            og = _sigmoid(o_ref[pl.ds(r0, L), hsl[h]].astype(f32))
            y_ref[pl.ds(r0, L), hsl[h]] = (og * hn).astype(y_ref.dtype)
            cta_ref[h] = new_cta[h]
        return 0

    lax.fori_loop(0, nc, chunk, 0, unroll=2)


def _mlstm(z, zg_t, col0, ml_conv_w, ml_conv_b, b_i, b_f, norm_g):
    b, t, _ = z.shape
    dm = norm_g.shape[0]
    nh = dm // HEAD_DIM
    taps = ml_conv_w.shape[0]
    assert taps - 1 <= ML_PAD

    def zspec(j):
        return pl.BlockSpec((None, t, dm), lambda bi: (bi, 0, col0 + j))

    return pl.pallas_call(
        _mlstm_kernel,
        grid=(b,),
        in_specs=[
            zspec(0), zspec(1), zspec(2), zspec(3),
            pl.BlockSpec((None, 2 * nh, t), lambda bi: (bi, 0, 0)),
            pl.BlockSpec((taps, 2 * dm), lambda bi: (0, 0)),
            pl.BlockSpec((1, 2 * dm), lambda bi: (0, 0)),
            pl.BlockSpec((nh, 1), lambda bi: (0, 0)),
            pl.BlockSpec((nh, 1), lambda bi: (0, 0)),
            pl.BlockSpec((1, dm), lambda bi: (0, 0)),
        ],
        out_specs=pl.BlockSpec((None, t, dm), lambda bi: (bi, 0, 0)),
        out_shape=jax.ShapeDtypeStruct((b, t, dm), bf16),
        scratch_shapes=[
            pltpu.VMEM((nh, t + ML_PAD, HEAD_DIM), f32),
            pltpu.VMEM((nh, t + ML_PAD, HEAD_DIM), f32),
            pltpu.VMEM((t, dm), bf16),
            pltpu.VMEM((t, dm), bf16),
            pltpu.VMEM((t, 2 * dm), bf16),
            pltpu.VMEM((nh, HEAD_DIM, 2 * HEAD_DIM), f32),
            pltpu.VMEM((nh, ML_CHUNK), f32),
        ],
        compiler_params=_cparams("parallel"),
        name="mlstm",
    )(z, z, z, z, zg_t, ml_conv_w, ml_conv_b.reshape(1, 2 * dm), b_i.reshape(nh, 1), b_f.reshape(nh, 1),
      norm_g.reshape(1, dm))


MASK_BIG = 1e30
BIAS_COLS = 128


def _moba_kernel(q_ref, k_ref, v_ref, kx_ref, sl_ref, ng_ref, y_ref, km_ref, s_ref, kc_ref, vc_ref):
    t, dh = q_ref.shape
    blk = MOBA_BLOCK
    nb = t // blk

    kc_ref[:, 0:dh] = k_ref[...]
    kc_ref[:, dh:dh + BIAS_COLS] = kx_ref[...]
    vc_ref[:, 0:dh] = v_ref[...]
    vc_ref[:, dh:2 * dh] = jnp.ones((t, dh), bf16)

    for n in range(nb):
        km_ref[n:n + 1, :] = jnp.mean(k_ref[n * blk:(n + 1) * blk, :].astype(f32), axis=0, keepdims=True)
    gate_t = _dot_nt(km_ref[...], q_ref[...].astype(f32), precision=HIGHEST)

    blk_id = lax.broadcasted_iota(jnp.int32, (nb, blk), 0)
    ri = lax.broadcasted_iota(jnp.int32, (blk, blk), 0)
    ci = lax.broadcasted_iota(jnp.int32, (blk, blk), 1)
    causal = ri >= ci
    eye = (ri == ci).astype(bf16)
    lane = lax.broadcasted_iota(jnp.int32, (blk, BIAS_COLS), 1)
    rowf = lax.broadcasted_iota(jnp.int32, (blk, BIAS_COLS), 0).astype(f32)
    slope = sl_ref[:, 0:1]

    def logits_pass(j):
        qj = q_ref[j * blk:(j + 1) * blk, :]
        masked_t = jnp.zeros((nb, blk), f32)
        if j > 0:
            g = jnp.where(blk_id < j, gate_t[:, j * blk:(j + 1) * blk], NEG)
            for n in range(j):
                gn = g[n:n + 1, :]
                ahead = (g > gn) | ((g == gn) & (blk_id < n))
                rank = jnp.sum(jnp.where(ahead, 1.0, 0.0), axis=0, keepdims=True)
                masked_t = jnp.where((blk_id == n) & (rank >= float(MOBA_TOPK)), -1.0, masked_t)
        masked_pad = jnp.concatenate([masked_t, jnp.zeros((BIAS_COLS - nb, blk), f32)], axis=0).astype(bf16)
        qx = _dot_nt(eye, masked_pad)
        qx = jnp.where(lane == nb, -slope * float(blk * j), qx)
        qx = jnp.where(lane == nb + 1, -slope * rowf, qx)
        qx = jnp.where((lane == nb + 2) | (lane == nb + 3), 1.0, qx).astype(bf16)
        qc = jnp.concatenate([qj, qx], axis=1)
        mx = None
        for n in range(j + 1):
            ks = slice(n * blk, (n + 1) * blk)
            lg = _dot_nt(qc, kc_ref[ks, :])
            if n == j:
                lg = jnp.where(causal, lg, NEG)
            s_ref[j % 2, :, ks] = lg
            half = jnp.maximum(lg[:, 0:LANE], lg[:, LANE:2 * LANE])
            mx = half if mx is None else jnp.maximum(mx, half)
        return jnp.max(mx, axis=-1, keepdims=True)

    def value_pass(j, m):
        acc = jnp.zeros((blk, 2 * dh), f32)
        for n in range(j + 1):
            ks = slice(n * blk, (n + 1) * blk)
            p = jnp.exp((s_ref[j % 2, :, ks] - m).astype(bf16))
            acc = acc + jnp.dot(p, vc_ref[ks, :], preferred_element_type=f32)
        o = acc[:, 0:dh] / acc[:, dh:2 * dh]
        o = o * lax.rsqrt(jnp.mean(o * o, axis=-1, keepdims=True) + EPS) * ng_ref[...]
        y_ref[j * blk:(j + 1) * blk, :] = o.astype(y_ref.dtype)

    m_prev = logits_pass(0)
    for j in range(1, nb):
        m_cur = logits_pass(j)
        value_pass(j - 1, m_prev)
        m_prev = m_cur
    value_pass(nb - 1, m_prev)


def _moba(z, col0, norm_g):
    b, t, _ = z.shape
    dh = HEAD_DIM
    nh = norm_g.shape[0] // dh
    blk = MOBA_BLOCK
    assert t % blk == 0
    nb = t // blk
    assert nb + 4 <= BIAS_COLS
    slopes = 2.0 ** (-(8.0 / nh) * jnp.arange(1, nh + 1, dtype=f32))
    pos = jnp.arange(t, dtype=jnp.int32)
    col = jnp.arange(BIAS_COLS, dtype=jnp.int32)[None, :]
    kblk = (pos // blk)[:, None]
    base = jnp.where(col == kblk, MASK_BIG, 0.0) + jnp.where((col == nb) | (col == nb + 1), 1.0, 0.0)
    per_head = (jnp.where(col == nb + 2, (kblk * blk).astype(f32), 0.0)
                + jnp.where(col == nb + 3, (pos % blk).astype(f32)[:, None], 0.0))
    kx = (base[None] + slopes[:, None, None] * per_head[None]).astype(bf16)
    slope_rows = jnp.broadcast_to(slopes[:, None, None], (nh, 1, LANE))

    def zspec(j):
        return pl.BlockSpec((None, t, dh), lambda bi, h: (bi, 0, col0 + j * nh + h))

    return pl.pallas_call(
        _moba_kernel,
        grid=(b, nh),
        in_specs=[
            zspec(0), zspec(1), zspec(2),
            pl.BlockSpec((None, t, BIAS_COLS), lambda bi, h: (h, 0, 0)),
            pl.BlockSpec((None, 1, LANE), lambda bi, h: (h, 0, 0)),
            pl.BlockSpec((None, 1, dh), lambda bi, h: (h, 0, 0)),
        ],
        out_specs=pl.BlockSpec((None, t, dh), lambda bi, h: (bi, 0, h)),
        out_shape=jax.ShapeDtypeStruct((b, t, nh * dh), bf16),
        scratch_shapes=[pltpu.VMEM((nb, dh), f32), pltpu.VMEM((2, blk, t), f32), pltpu.VMEM((t, dh + BIAS_COLS), bf16),
                        pltpu.VMEM((t, 2 * dh), bf16)],
        compiler_params=_cparams("parallel", "parallel"),
        name="moba_attention",
    )(z, z, z, kx, slope_rows, norm_g.reshape(nh, 1, dh))


def _router_logits_t(w_t32, h32, hb):
    h_lo = (h32 - hb.astype(f32)).astype(bf16)
    w_hi = w_t32.astype(bf16)
    w_lo = (w_t32 - w_hi.astype(f32)).astype(bf16)
    return _dot_nt(w_hi, hb) + _dot_nt(w_hi, h_lo) + _dot_nt(w_lo, hb)


def _top2_t(logits):
    ne = logits.shape[0]
    eid = lax.broadcasted_iota(jnp.int32, logits.shape, 0)
    m1 = jnp.max(logits, axis=0, keepdims=True)
    i1 = jnp.min(jnp.where(logits == m1, eid, ne), axis=0, keepdims=True)
    rest = jnp.where(eid == i1, -jnp.inf, logits)
    m2 = jnp.max(rest, axis=0, keepdims=True)
    i2 = jnp.min(jnp.where(rest == m2, eid, ne), axis=0, keepdims=True)
    e2 = jnp.exp(m2 - m1)
    p1 = 1.0 / (1.0 + e2)
    return i1, i2, p1, e2 * p1


OUTPROJ_SPLIT = 2


def _outproj_kernel(yc_ref, ym_ref, ya_ref, w_ref, x_ref, gt_ref, g_ref, sh_ref, sc_ref, *rest, routed):
    if routed:
        wr_ref, br_ref, xo_ref, h_ref, idx_ref, p_ref = rest
    else:
        xo_ref, h_ref = rest
    tm = x_ref.shape[0]
    sub = tm // OUTPROJ_SPLIT
    blocks = [slice(r * sub, (r + 1) * sub) for r in range(OUTPROJ_SPLIT)]
    accs = [jnp.dot(jnp.concatenate([yc_ref[rs, :], ym_ref[rs, :], ya_ref[rs, :]], axis=1), w_ref[...],
                    preferred_element_type=f32) for rs in blocks]
    for rs, acc in zip(blocks, accs):
        xn = x_ref[rs, :] + gt_ref[...] * acc
        xo_ref[rs, :] = xn
        h = _norm_modulate(xn, g_ref[...], sh_ref[...], sc_ref[...])
        hb = h.astype(bf16)
        h_ref[rs, :] = h if routed else hb
        if routed:
            i1, i2, p1, p2 = _top2_t(_router_logits_t(wr_ref[...], h, hb) + br_ref[...])
            first = lax.broadcasted_iota(jnp.int32, (TOP_K, sub), 0) == 0
            idx_ref[:, rs] = jnp.where(first, i1, i2)
            p_ref[:, rs] = jnp.where(first, p1, p2)


def _out_projection(y_conv, y_ml, y_att, w_out, x, mod4, layer, g_ffn, router=None, tm=512):
    b, t, d = x.shape
    c1, c2, c3 = y_conv.shape[-1], y_ml.shape[-1], y_att.shape[-1]
    assert c1 + c2 + c3 == w_out.shape[0]
    row = lambda c: pl.BlockSpec((None, tm, c), lambda bi, i: (bi, i, 0))
    modspec = lambda chunk: pl.BlockSpec((None, None, 1, d), lambda bi, i: (layer, bi, 0, chunk))
    in_specs = [
        row(c1), row(c2), row(c3),
        pl.BlockSpec(w_out.shape, lambda bi, i: (0, 0)),
        row(d), modspec(2),
        pl.BlockSpec((1, d), lambda bi, i: (0, 0)), modspec(3), modspec(4),
    ]
    args = [y_conv, y_ml, y_att, w_out, x, mod4, g_ffn.reshape(1, d), mod4, mod4]
    out_specs = [row(d), row(d)]
    out_shape = [jax.ShapeDtypeStruct((b, t, d), f32), jax.ShapeDtypeStruct((b, t, d), bf16 if router is None else f32)]
    if router is not None:
        w_r, b_r = router
        ne = w_r.shape[1]
        in_specs += [pl.BlockSpec((ne, d), lambda bi, i: (0, 0)), pl.BlockSpec((ne, 1), lambda bi, i: (0, 0))]
        args += [w_r.T, b_r.reshape(ne, 1)]
        kspec = pl.BlockSpec((None, TOP_K, tm), lambda bi, i: (bi, 0, i))
        out_specs += [kspec, kspec]
        out_shape += [jax.ShapeDtypeStruct((b, TOP_K, t), jnp.int32), jax.ShapeDtypeStruct((b, TOP_K, t), f32)]
    return pl.pallas_call(
        functools.partial(_outproj_kernel, routed=router is not None),
        grid=(b, t // tm),
        in_specs=in_specs,
        out_specs=out_specs,
        out_shape=out_shape,
        compiler_params=_cparams("parallel", "parallel"),
        name="out_projection",
    )(*args)


def _swiglu_rows(hb, wg_ref, wu_ref, wd_ref):
    g = jnp.dot(hb, wg_ref[...].astype(bf16), preferred_element_type=f32)
    u = jnp.dot(hb, wu_ref[...].astype(bf16), preferred_element_type=f32)
    a = (_silu(g) * u).astype(bf16)
    return jnp.dot(a, wd_ref[...].astype(bf16), preferred_element_type=f32)


def _dense_swiglu_kernel(h_ref, wg_ref, wu_ref, wd_ref, o_ref, acc_ref):
    f = pl.program_id(1)

    @pl.when(f == 0)
    def _():
        acc_ref[...] = jnp.zeros_like(acc_ref)

    acc_ref[...] += _swiglu_rows(h_ref[...], wg_ref, wu_ref, wd_ref)

    @pl.when(f == pl.num_programs(1) - 1)
    def _():
        o_ref[...] = acc_ref[...].astype(o_ref.dtype)


def _dense_swiglu(h, wg, wu, wd, layer, tm=1024, tf=512):
    n, d = h.shape
    ff = wg.shape[-1]
    return pl.pallas_call(
        _dense_swiglu_kernel,
        grid=(n // tm, ff // tf),
        in_specs=[
            pl.BlockSpec((tm, d), lambda i, f: (i, 0)),
            pl.BlockSpec((None, d, tf), lambda i, f: (layer, 0, f)),
            pl.BlockSpec((None, d, tf), lambda i, f: (layer, 0, f)),
            pl.BlockSpec((None, tf, d), lambda i, f: (layer, f, 0)),
        ],
        out_specs=pl.BlockSpec((tm, d), lambda i, f: (i, 0)),
        out_shape=jax.ShapeDtypeStruct((n, d), bf16),
        scratch_shapes=[pltpu.VMEM((tm, d), f32)],
        compiler_params=_cparams("parallel", "arbitrary"),
        name="dense_swiglu",
    )(h, wg, wu, wd)


ROW_UNIT = 128
GATHER_UNROLL = 8


def _moe_swiglu_kernel(te_ref, nq_ref, tok_ref, tok_next_ref, h_hbm, wg_ref, wu_ref, wd_ref, o_ref,
                       hrows_ref, hb_ref, acc_ref, sem):
    i = pl.program_id(0)
    f = pl.program_id(1)
    nt = pl.num_programs(0)
    nq = nq_ref[i]
    tm = acc_ref.shape[0]

    def row_copy(src_row, dst_row, rows=1):
        return pltpu.make_async_copy(h_hbm.at[pl.ds(src_row, rows)], hrows_ref.at[pl.ds(dst_row, rows)], sem)

    def start_gather(tokens_ref, units):
        def issue(g, _):
            for u in range(GATHER_UNROLL):
                r = g * GATHER_UNROLL + u
                row_copy(tokens_ref[0, r], r).start()
            return 0

        lax.fori_loop(0, units * (ROW_UNIT // GATHER_UNROLL), issue, 0)

    @pl.when(f == 0)
    def _():
        acc_ref[...] = jnp.zeros_like(acc_ref)

        @pl.when((i == 0) & (nq > 0))
        def _():
            start_gather(tok_ref, nq)

        @pl.when(nq > 0)
        def _():
            row_copy(0, 0, nq * ROW_UNIT).wait()
            for q in range(tm // ROW_UNIT):
                @pl.when(q < nq)
                def _(q=q):
                    qs = slice(q * ROW_UNIT, (q + 1) * ROW_UNIT)
                    hb_ref[qs, :] = hrows_ref[qs, :].astype(bf16)

        nq_next = nq_ref[jnp.minimum(i + 1, nt - 1)]

        @pl.when((i + 1 < nt) & (nq_next > 0))
        def _():
            start_gather(tok_next_ref, nq_next)

    for q in range(1, tm // ROW_UNIT + 1):
        @pl.when(nq == q)
        def _(q=q):
            rows = q * ROW_UNIT
            acc_ref[0:rows, :] += _swiglu_rows(hb_ref[0:rows, :], wg_ref, wu_ref, wd_ref)

    @pl.when(f == pl.num_programs(1) - 1)
    def _():
        o_ref[...] = acc_ref[...].astype(o_ref.dtype)


def _moe_swiglu(h32, row_token, tile_expert, tile_units, wg, wu, wd, tm=1024, tf=256):
    n, d = h32.shape
    nt = row_token.shape[0]
    ff = wg.shape[-1]
    nf = ff // tf
    assert tm % ROW_UNIT == 0

    def wmap(i, f, te, nq):
        return (te[i], 0, jnp.where(nq[i] > 0, f, nf - 1))

    def wdmap(i, f, te, nq):
        return (te[i], jnp.where(nq[i] > 0, f, nf - 1), 0)

    grid_spec = pltpu.PrefetchScalarGridSpec(
        num_scalar_prefetch=2,
        grid=(nt, nf),
        in_specs=[
            pl.BlockSpec((None, 1, tm), lambda i, f, te, nq: (i, 0, 0), memory_space=pltpu.SMEM),
            pl.BlockSpec((None, 1, tm), lambda i, f, te, nq: (jnp.minimum(i + 1, nt - 1), 0, 0),
                         memory_space=pltpu.SMEM),
            pl.BlockSpec(memory_space=pl.ANY),
            pl.BlockSpec((None, d, tf), wmap),
            pl.BlockSpec((None, d, tf), wmap),
            pl.BlockSpec((None, tf, d), wdmap),
        ],
        out_specs=pl.BlockSpec((tm, d), lambda i, f, te, nq: (i, 0)),
        scratch_shapes=[
            pltpu.VMEM((tm, d), f32),
            pltpu.VMEM((tm, d), bf16),
            pltpu.VMEM((tm, d), f32),
            pltpu.SemaphoreType.DMA,
        ],
    )
    return pl.pallas_call(
        _moe_swiglu_kernel,
        grid_spec=grid_spec,
        out_shape=jax.ShapeDtypeStruct((nt * tm, d), bf16),
        compiler_params=_cparams("arbitrary", "arbitrary"),
        name="moe_swiglu",
    )(tile_expert, tile_units, row_token, row_token, h32, wg, wu, wd)


def _combine_kernel(*refs, n_y, weighted, last):
    x_ref, gt_ref = refs[0], refs[1]
    ys = [r[...].astype(f32) for r in refs[2:2 + n_y]]
    rest = refs[2 + n_y:]
    if weighted:
        p = rest[0][...]
        rest = rest[1:]
        ys = [p[:, k:k + 1] * y for k, y in enumerate(ys)]
    y = ys[0]
    for yk in ys[1:]:
        y = y + yk
    xn = x_ref[...] + gt_ref[...] * y
    if last:
        g_ref, o_ref = rest
        o_ref[...] = xn * lax.rsqrt(jnp.mean(xn * xn, axis=-1, keepdims=True) + EPS) * g_ref[...]
    else:
        g_ref, sh_ref, sc_ref, xo_ref, h_ref = rest
        xo_ref[...] = xn
        h_ref[...] = _norm_modulate(xn, g_ref[...], sh_ref[...], sc_ref[...]).astype(h_ref.dtype)


def _combine(x, mod4, layer, ys, probs, g_next, last, tm=512):
    b, t, d = x.shape
    spec = pl.BlockSpec((None, tm, d), lambda bi, i: (bi, i, 0))
    modspec = lambda lyr, chunk: pl.BlockSpec((None, None, 1, d), lambda bi, i: (lyr, bi, 0, chunk))
    in_specs = [spec, modspec(layer, 5)] + [spec] * len(ys)
    args = [x, mod4, *ys]
    if probs is not None:
        in_specs.append(pl.BlockSpec((None, tm, len(ys)), lambda bi, i: (bi, i, 0)))
        args.append(probs)
    in_specs.append(pl.BlockSpec((1, d), lambda bi, i: (0, 0)))
    args.append(g_next.reshape(1, d))
    if last:
        out_specs, out_shape = spec, jax.ShapeDtypeStruct((b, t, d), f32)
    else:
        in_specs += [modspec(layer + 1, 0), modspec(layer + 1, 1)]
        args += [mod4, mod4]
        out_specs = [spec, spec]
        out_shape = [jax.ShapeDtypeStruct((b, t, d), f32), jax.ShapeDtypeStruct((b, t, d), bf16)]
    return pl.pallas_call(
        functools.partial(_combine_kernel, n_y=len(ys), weighted=probs is not None, last=last),
        grid=(b, t // tm),
        in_specs=in_specs,
        out_specs=out_specs,
        out_shape=out_shape,
        compiler_params=_cparams("parallel", "parallel"),
        name="ffn_combine",
    )(*args)


def _routing_tables(idx, n_experts, tm):
    n = idx.shape[0]
    na = n * TOP_K
    nt = na // tm + n_experts
    e_flat = idx.reshape(na)
    onehot = (e_flat[:, None] == jnp.arange(n_experts, dtype=jnp.int32)[None, :]).astype(jnp.int32)
    counts = jnp.sum(onehot, axis=0)
    rank = jnp.sum((jnp.cumsum(onehot, axis=0) - 1) * onehot, axis=1)
    tiles_per = (counts + tm - 1) // tm
    tile_end = jnp.cumsum(tiles_per)
    tile_start = tile_end - tiles_per
    slot = tile_start[e_flat] * tm + rank
    n_valid = tile_end[-1:].astype(jnp.int32)
    tile_ids = jnp.arange(nt, dtype=jnp.int32)
    tile_expert = jnp.sum((tile_ids[:, None] >= tile_end[None, :]).astype(jnp.int32), axis=1)
    last_expert = jnp.sum((n_valid - 1 >= tile_end).astype(jnp.int32))
    tile_expert = jnp.where(tile_ids < n_valid, tile_expert, last_expert).astype(jnp.int32)
    tile_rows = jnp.clip(counts[tile_expert] - (tile_ids - tile_start[tile_expert]) * tm, 0, tm)
    tile_units = jnp.where(tile_ids < n_valid, (tile_rows + ROW_UNIT - 1) // ROW_UNIT, 0).astype(jnp.int32)
    token = jnp.arange(na, dtype=jnp.int32) // TOP_K
    row_token = jnp.zeros((nt * tm,), jnp.int32).at[slot].set(token, unique_indices=True, mode="promise_in_bounds")
    return slot.reshape(n, TOP_K), row_token.reshape(nt, 1, tm), tile_expert, tile_units


def kernel(x, c, w_mod, b_mod, g_mix, g_ffn, w_in, conv_w, conv_b, conv_ln_g, conv_ln_b, ml_conv_w, ml_conv_b, ml_b_i, ml_b_f, ml_norm_g, attn_norm_g, w_out, ffn_w_gate, ffn_w_up, ffn_w_down, moe_w_router, moe_b_router, moe_w_gate, moe_w_up, moe_w_down, g_final):
    b, t, d = x.shape
    depth = w_mod.shape[0]
    conv_ch = conv_w.shape[-1]
    ml_dim = ml_norm_g.shape[-1]
    ml_heads = ml_b_i.shape[-1]
    n_experts = moe_w_router.shape[-1]
    att_dim = attn_norm_g.shape[-1]
    n_tok = b * t
    tm_ffn = 1024

    mod = _adaln_mod(c, w_mod, b_mod)
    mod4 = mod.reshape(depth, b, 1, 6 * d)

    gate0 = 2 * conv_ch + 4 * ml_dim
    gate1 = gate0 + 2 * ml_heads
    w_in_t = jnp.swapaxes(w_in, 1, 2)

    h = _prep(x, g_mix[0], mod4, 0, 0)
    out = None
    for l in range(depth):
        z, zg = _in_projection(h.reshape(n_tok, d), w_in_t, l, gate0, gate1, att_dim, HEAD_DIM ** -0.5)
        z = z.reshape(b, t, -1)
        zg = zg.reshape(b, t, -1)
        zg_t = jnp.swapaxes(zg, 1, 2)

        y_conv = _conv_module(z, conv_w[l], conv_b[l], conv_ln_g[l], conv_ln_b[l])
        y_ml = _mlstm(z, zg_t, (2 * conv_ch) // ml_dim, ml_conv_w[l], ml_conv_b[l], ml_b_i[l], ml_b_f[l], ml_norm_g[l])
        y_att = _moba(z, gate0 // HEAD_DIM, attn_norm_g[l])

        j = l // 2
        w_o = w_out[l].astype(bf16)
        if l % 2 == 0:
            x, h = _out_projection(y_conv, y_ml, y_att, w_o, x, mod4, l, g_ffn[l])
            y = _dense_swiglu(h.reshape(n_tok, d), ffn_w_gate, ffn_w_up, ffn_w_down, j, tm=tm_ffn)
            ys, y_probs = [y.reshape(b, t, d)], None
        else:
            x, h32, idx, probs = _out_projection(y_conv, y_ml, y_att, w_o, x, mod4, l, g_ffn[l],
                                                 router=(moe_w_router[j], moe_b_router[j]))
            slot, row_token, tile_expert, tile_units = _routing_tables(
                jnp.swapaxes(idx, 1, 2).reshape(n_tok, TOP_K), n_experts, tm_ffn)
            ff = moe_w_gate.shape[-1]
            y_sorted = _moe_swiglu(
                h32.reshape(n_tok, d), row_token, tile_expert + j * n_experts, tile_units,
                moe_w_gate.reshape(-1, d, ff), moe_w_up.reshape(-1, d, ff), moe_w_down.reshape(-1, ff, d), tm=tm_ffn)
            ys = [y_sorted.at[slot[:, k]].get(mode="promise_in_bounds").reshape(b, t, d) for k in range(TOP_K)]
            y_probs = jnp.swapaxes(probs, 1, 2)

        if l + 1 < depth:
            x, h = _combine(x, mod4, l, ys, y_probs, g_mix[l + 1], last=False)
        else:
            out = _combine(x, mod4, l, ys, y_probs, g_final, last=True)
    return out
```

```python
import functools

import jax
import jax.numpy as jnp
from jax import lax
from jax.experimental import pallas as pl
from jax.experimental.pallas import tpu as pltpu

f32 = jnp.float32
bf16 = jnp.bfloat16
HIGHEST = lax.Precision.HIGHEST

CONV_GROUPS = 4
HEAD_DIM = 128
ML_CHUNK = 128
MOBA_BLOCK = 256
MOBA_TOPK = 3
TOP_K = 2
EPS = 1e-6
NEG = -1e30

VMEM_LIMIT_BYTES = 56 * 1024 * 1024
LANE = 128


def _cparams(*sem):
    return pltpu.CompilerParams(dimension_semantics=sem, vmem_limit_bytes=VMEM_LIMIT_BYTES)


def _sigmoid(v):
    return 1.0 / (1.0 + jnp.exp(-v))


def _silu(v):
    return v * _sigmoid(v)


def _dot_nt(a, b, **kw):
    return lax.dot_general(a, b, (((1,), (1,)), ((), ())), preferred_element_type=f32, **kw)


def _mod_kernel(c_ref, w_ref, b_ref, o_ref):
    cond = _silu(c_ref[...]).astype(bf16)
    o_ref[...] = jnp.dot(cond, w_ref[...].astype(bf16), preferred_element_type=f32) + b_ref[...]


def _adaln_mod(c, w_mod, b_mod, tn=1024):
    depth, d, n6 = w_mod.shape
    b = c.shape[0]
    return pl.pallas_call(
        _mod_kernel,
        grid=(depth, n6 // tn),
        in_specs=[
            pl.BlockSpec((b, d), lambda l, n: (0, 0)),
            pl.BlockSpec((None, d, tn), lambda l, n: (l, 0, n)),
            pl.BlockSpec((None, 1, tn), lambda l, n: (l, 0, n)),
        ],
        out_specs=pl.BlockSpec((None, b, tn), lambda l, n: (l, 0, n)),
        out_shape=jax.ShapeDtypeStruct((depth, b, n6), f32),
        compiler_params=_cparams("parallel", "parallel"),
        name="adaln_mod",
    )(c, w_mod, b_mod.reshape(depth, 1, n6))


def _norm_modulate(x, g, sh, sc):
    ms = jnp.mean(x * x, axis=-1, keepdims=True)
    h = x * lax.rsqrt(ms + EPS) * g
    return h * (1.0 + sc) + sh


def _prep_kernel(x_ref, g_ref, sh_ref, sc_ref, h_ref):
    h_ref[...] = _norm_modulate(x_ref[...], g_ref[...], sh_ref[...], sc_ref[...]).astype(h_ref.dtype)


def _prep(x, g, mod4, layer, chunk, tm=512):
    b, t, d = x.shape
    return pl.pallas_call(
        _prep_kernel,
        grid=(b, t // tm),
        in_specs=[
            pl.BlockSpec((None, tm, d), lambda bi, i: (bi, i, 0)),
            pl.BlockSpec((1, d), lambda bi, i: (0, 0)),
            pl.BlockSpec((None, None, 1, d), lambda bi, i: (layer, bi, 0, chunk)),
            pl.BlockSpec((None, None, 1, d), lambda bi, i: (layer, bi, 0, chunk + 1)),
        ],
        out_specs=pl.BlockSpec((None, tm, d), lambda bi, i: (bi, i, 0)),
        out_shape=jax.ShapeDtypeStruct((b, t, d), bf16),
        compiler_params=_cparams("parallel", "parallel"),
        name="prep",
    )(x, g.reshape(1, d), mod4, mod4)


def _inproj_kernel(h_ref, wa_ref, wb_ref, wg_ref, z_ref, zg_ref, *, n_a, n_q, q_scale):
    j = pl.program_id(1)

    @pl.when(j == 0)
    def _():
        zg_ref[...] = _dot_nt(h_ref[...], wg_ref[...].astype(bf16))

    @pl.when(j < n_a)
    def _():
        z_ref[...] = _dot_nt(h_ref[...], wa_ref[...].astype(bf16)).astype(bf16)

    @pl.when((j >= n_a) & (j < n_a + n_q))
    def _():
        z_ref[...] = (_dot_nt(h_ref[...], wb_ref[0].astype(bf16)) * q_scale).astype(bf16)

    @pl.when(j >= n_a + n_q)
    def _():
        z_ref[...] = _dot_nt(h_ref[...], wb_ref[0].astype(bf16)).astype(bf16)


def _in_projection(h, w_t, layer, gate0, gate1, q_features, q_scale, tm=2048, tn=512):
    n, d = h.shape
    n_in = w_t.shape[1]
    ng = gate1 - gate0
    n_a = gate0 // tn
    n_b = (n_in - gate1) // tn
    n_q = q_features // tn
    assert n_a * tn == gate0 and n_b * tn == n_in - gate1 and n_q * tn == q_features
    assert gate0 % ng == 0 and gate1 % 8 == 0
    return pl.pallas_call(
        functools.partial(_inproj_kernel, n_a=n_a, n_q=n_q, q_scale=q_scale),
        grid=(n // tm, n_a + n_b),
        in_specs=[
            pl.BlockSpec((tm, d), lambda i, j: (i, 0)),
            pl.BlockSpec((None, tn, d), lambda i, j: (layer, jnp.minimum(j, n_a - 1), 0)),
            pl.BlockSpec((pl.Element(1), pl.Element(tn), pl.Element(d)),
                         lambda i, j: (layer, pl.multiple_of(gate1 + tn * jnp.maximum(j - n_a, 0), 8), 0)),
            pl.BlockSpec((None, ng, d), lambda i, j: (layer, gate0 // ng, 0)),
        ],
        out_specs=[
            pl.BlockSpec((tm, tn), lambda i, j: (i, j)),
            pl.BlockSpec((tm, ng), lambda i, j: (i, 0)),
        ],
        out_shape=[
            jax.ShapeDtypeStruct((n, (n_a + n_b) * tn), bf16),
            jax.ShapeDtypeStruct((n, ng), f32),
        ],
        compiler_params=_cparams("parallel", "arbitrary"),
        name="in_projection",
    )(h, w_t, w_t, w_t)


CONV_PAD = 32
ROWS = 128


def _conv_kernel(ag_ref, w_ref, b_ref, lg_ref, lb_ref, o_ref, xp_ref):
    t, c = o_ref.shape
    taps = w_ref.shape[0]
    gc = c // CONV_GROUPS
    for cg in range(CONV_GROUPS):
        cs = slice(cg * gc, (cg + 1) * gc)
        gs = slice(c + cg * gc, c + (cg + 1) * gc)
        xp_ref[cg, 0:CONV_PAD, :] = jnp.zeros((CONV_PAD, gc), f32)

        def glu(i, _, cg=cg, cs=cs, gs=gs):
            r0 = pl.multiple_of(i * ROWS, ROWS)
            a = ag_ref[pl.ds(r0, ROWS), cs].astype(f32)
            g = ag_ref[pl.ds(r0, ROWS), gs].astype(f32)
            xp_ref[cg, pl.ds(CONV_PAD + r0, ROWS), :] = a * _sigmoid(g)
            return 0

        lax.fori_loop(0, t // ROWS, glu, 0)

        def body(i, _, cg=cg, cs=cs):
            r0 = pl.multiple_of(i * ROWS, ROWS)
            acc = jnp.zeros((ROWS, gc), f32) + b_ref[:, cs]
            for k in range(taps):
                acc = acc + xp_ref[cg, pl.ds(r0 + (CONV_PAD - (taps - 1) + k), ROWS), :] * w_ref[k:k + 1, cs]
            mu = jnp.mean(acc, axis=-1, keepdims=True)
            dv = acc - mu
            var = jnp.mean(dv * dv, axis=-1, keepdims=True)
            y = dv * lax.rsqrt(var + EPS) * lg_ref[:, cs] + lb_ref[:, cs]
            o_ref[pl.ds(r0, ROWS), cs] = _silu(y).astype(o_ref.dtype)
            return 0

        lax.fori_loop(0, t // ROWS, body, 0, unroll=4)


def _conv_module(z, conv_w, conv_b, ln_g, ln_b):
    b, t, _ = z.shape
    taps, c = conv_w.shape
    assert taps - 1 <= CONV_PAD and c // CONV_GROUPS == LANE
    return pl.pallas_call(
        _conv_kernel,
        grid=(b,),
        in_specs=[
            pl.BlockSpec((None, t, 2 * c), lambda bi: (bi, 0, 0)),
            pl.BlockSpec((taps, c), lambda bi: (0, 0)),
            pl.BlockSpec((1, c), lambda bi: (0, 0)),
            pl.BlockSpec((1, c), lambda bi: (0, 0)),
            pl.BlockSpec((1, c), lambda bi: (0, 0)),
        ],
        out_specs=pl.BlockSpec((None, t, c), lambda bi: (bi, 0, 0)),
        out_shape=jax.ShapeDtypeStruct((b, t, c), bf16),
        scratch_shapes=[pltpu.VMEM((CONV_GROUPS, t + CONV_PAD, c // CONV_GROUPS), f32)],
        compiler_params=_cparams("parallel"),
        name="conv_module",
    )(z, conv_w, conv_b.reshape(1, c), ln_g.reshape(1, c), ln_b.reshape(1, c))


ML_PAD = 8


def _mlstm_kernel(q_ref, k_ref, v_ref, o_ref, gr_ref, cw_ref, cb_ref, bic_ref, bfc_ref,
                  ng_ref, y_ref, qp_ref, kp_ref, qs_ref, ks_ref, va_ref, cta_ref, m_ref):
    t, dm = q_ref.shape
    nh = dm // HEAD_DIM
    L = ML_CHUNK
    nc = t // L
    taps = cw_ref.shape[0]
    kscale = HEAD_DIM ** -0.5

    for h in range(nh):
        hs = slice(h * HEAD_DIM, (h + 1) * HEAD_DIM)
        ks_cols = slice(dm + h * HEAD_DIM, dm + (h + 1) * HEAD_DIM)
        qp_ref[h, 0:ML_PAD, :] = jnp.zeros((ML_PAD, HEAD_DIM), f32)
        kp_ref[h, 0:ML_PAD, :] = jnp.zeros((ML_PAD, HEAD_DIM), f32)

        def stage(i, _, h=h, hs=hs):
            r0 = pl.multiple_of(i * ROWS, ROWS)
            qp_ref[h, pl.ds(ML_PAD + r0, ROWS), :] = q_ref[pl.ds(r0, ROWS), hs].astype(f32)
            kp_ref[h, pl.ds(ML_PAD + r0, ROWS), :] = k_ref[pl.ds(r0, ROWS), hs].astype(f32)
            return 0

        lax.fori_loop(0, t // ROWS, stage, 0)

        def sconv(i, _, h=h, hs=hs, ks_cols=ks_cols):
            r0 = pl.multiple_of(i * ROWS, ROWS)
            aq = jnp.zeros((ROWS, HEAD_DIM), f32) + cb_ref[:, hs]
            ak = jnp.zeros((ROWS, HEAD_DIM), f32) + cb_ref[:, ks_cols]
            for j in range(taps):
                off = ML_PAD - (taps - 1) + j
                aq = aq + qp_ref[h, pl.ds(r0 + off, ROWS), :] * cw_ref[j:j + 1, hs]
                ak = ak + kp_ref[h, pl.ds(r0 + off, ROWS), :] * cw_ref[j:j + 1, ks_cols]
            qs_ref[pl.ds(r0, ROWS), hs] = _silu(aq).astype(bf16)
            ks_ref[pl.ds(r0, ROWS), hs] = (_silu(ak) * kscale).astype(bf16)
            return 0

        lax.fori_loop(0, t // ROWS, sconv, 0, unroll=2)

    for h in range(nh):
        hs = slice(h * HEAD_DIM, (h + 1) * HEAD_DIM)
        va_ref[:, 2 * h * HEAD_DIM:(2 * h + 1) * HEAD_DIM] = v_ref[:, hs]
        va_ref[:, (2 * h + 1) * HEAD_DIM:(2 * h + 2) * HEAD_DIM] = jnp.ones((t, HEAD_DIM), bf16)

    ri = lax.broadcasted_iota(jnp.int32, (L, L), 0)
    ci = lax.broadcasted_iota(jnp.int32, (L, L), 1)
    causal = ri >= ci
    lower_f = causal.astype(f32)
    eye_f = (ri == ci).astype(f32)
    eye_b = eye_f.astype(bf16)
    upper_b = (ri <= ci).astype(bf16)
    ones_b = jnp.ones((L, HEAD_DIM), bf16)
    lane = lax.broadcasted_iota(jnp.int32, (nh, L), 1)

    def log_sigmoid(v):
        return jnp.minimum(v, 0.0) - jnp.log(1.0 + jnp.exp(-jnp.abs(v)))

    def split(a, pieces):
        out = []
        for _ in range(pieces - 1):
            p = a.astype(bf16)
            out.append(p)
            a = a - p.astype(f32)
        out.append(a.astype(bf16))
        return out

    def rows_to_columns(weighted):
        return sum(jnp.dot(p, ones_b, preferred_element_type=f32) for p in split(weighted, 2))

    cta_ref[...] = jnp.zeros_like(cta_ref)
    m_ref[...] = jnp.zeros_like(m_ref)

    def chunk(i, _):
        r0 = pl.multiple_of(i * L, L)
        li = gr_ref[0:nh, pl.ds(r0, L)] + bic_ref[...]
        lf = log_sigmoid(gr_ref[nh:2 * nh, pl.ds(r0, L)] + bfc_ref[...])
        b = sum(jnp.dot(p, upper_b, preferred_element_type=f32) for p in split(lf, 3))
        g = li - b
        cmax = g
        for sh in (1, 2, 4, 8, 16, 32, 64):
            cmax = jnp.maximum(cmax, jnp.where(lane >= sh, pltpu.roll(cmax, sh, axis=1), NEG))
        m_prev = m_ref[...]
        big_m = jnp.maximum(cmax, m_prev)
        m_last = big_m[:, L - 1:L]
        b_last = b[:, L - 1:L]
        wk = jnp.exp(g - m_last)
        decay = jnp.exp(m_prev[:, 0:1] - m_last)
        m_ref[...] = jnp.broadcast_to(b_last + m_last, (nh, L))

        heads = range(nh)
        hsl = [slice(h * HEAD_DIM, (h + 1) * HEAD_DIM) for h in heads]
        q = [qs_ref[pl.ds(r0, L), hsl[h]] for h in heads]
        k = [ks_ref[pl.ds(r0, L), hsl[h]] for h in heads]
        va = [va_ref[pl.ds(r0, L), 2 * h * HEAD_DIM:(2 * h + 2) * HEAD_DIM] for h in heads]
        cta = [cta_ref[h] for h in heads]
        m_col = [rows_to_columns(eye_f * big_m[h:h + 1, :]) for h in heads]
        b_col = [rows_to_columns(lower_f * lf[h:h + 1, :]) for h in heads]
        qk = [_dot_nt(q[h], k[h]) for h in heads]
        k_t = [_dot_nt(eye_b, k[h]) for h in heads]
        inter = [jnp.dot(q[h], cta[h].astype(bf16), preferred_element_type=f32) for h in heads]
        s = [(qk[h] * jnp.exp(jnp.where(causal, g[h:h + 1, :] - m_col[h], NEG))).astype(bf16) for h in heads]
        kw_t = [(k_t[h] * wk[h:h + 1, :]).astype(bf16) for h in heads]
        w_inter = [jnp.exp(m_prev[h:h + 1, 0:1] - m_col[h]) for h in heads]
        both = [jnp.dot(s[h], va[h], preferred_element_type=f32)
                + jnp.concatenate([w_inter[h], w_inter[h]], axis=1) * inter[h] for h in heads]
        new_cta = [decay[h:h + 1, :] * cta[h] + jnp.dot(kw_t[h], va[h], preferred_element_type=f32) for h in heads]
        hout = [both[h][:, 0:HEAD_DIM] / jnp.maximum(jnp.abs(both[h][:, HEAD_DIM:2 * HEAD_DIM]),
                                                     jnp.exp(-(b_col[h] + m_col[h]))) for h in heads]
        ssq = [sum(jnp.dot(pc, ones_b, preferred_element_type=f32) for pc in split(hout[h] * hout[h], 2)) for h in heads]
        for h in heads:
            hn = hout[h] * lax.rsqrt(ssq[h] * (1.0 / HEAD_DIM) + EPS) * ng_ref[:, hsl[h]]
            og = _sigmoid(o_ref[pl.ds(r0, L), hsl[h]].astype(f32))
            y_ref[pl.ds(r0, L), hsl[h]] = (og * hn).astype(y_ref.dtype)
            cta_ref[h] = new_cta[h]
        return 0

    lax.fori_loop(0, nc, chunk, 0, unroll=2)


def _mlstm(z, zg_t, col0, ml_conv_w, ml_conv_b, b_i, b_f, norm_g):
    b, t, _ = z.shape
    dm = norm_g.shape[0]
    nh = dm // HEAD_DIM
    taps = ml_conv_w.shape[0]
    assert taps - 1 <= ML_PAD

    def zspec(j):
        return pl.BlockSpec((None, t, dm), lambda bi: (bi, 0, col0 + j))

    return pl.pallas_call(
        _mlstm_kernel,
        grid=(b,),
        in_specs=[
            zspec(0), zspec(1), zspec(2), zspec(3),
            pl.BlockSpec((None, 2 * nh, t), lambda bi: (bi, 0, 0)),
            pl.BlockSpec((taps, 2 * dm), lambda bi: (0, 0)),
            pl.BlockSpec((1, 2 * dm), lambda bi: (0, 0)),
            pl.BlockSpec((nh, 1), lambda bi: (0, 0)),
            pl.BlockSpec((nh, 1), lambda bi: (0, 0)),
            pl.BlockSpec((1, dm), lambda bi: (0, 0)),
        ],
        out_specs=pl.BlockSpec((None, t, dm), lambda bi: (bi, 0, 0)),
        out_shape=jax.ShapeDtypeStruct((b, t, dm), bf16),
        scratch_shapes=[
            pltpu.VMEM((nh, t + ML_PAD, HEAD_DIM), f32),
            pltpu.VMEM((nh, t + ML_PAD, HEAD_DIM), f32),
            pltpu.VMEM((t, dm), bf16),
            pltpu.VMEM((t, dm), bf16),
            pltpu.VMEM((t, 2 * dm), bf16),
            pltpu.VMEM((nh, HEAD_DIM, 2 * HEAD_DIM), f32),
            pltpu.VMEM((nh, ML_CHUNK), f32),
        ],
        compiler_params=_cparams("parallel"),
        name="mlstm",
    )(z, z, z, z, zg_t, ml_conv_w, ml_conv_b.reshape(1, 2 * dm), b_i.reshape(nh, 1), b_f.reshape(nh, 1),
      norm_g.reshape(1, dm))


MASK_BIG = 1e30
BIAS_COLS = 128


def _moba_kernel(q_ref, k_ref, v_ref, kx_ref, sl_ref, ng_ref, y_ref, km_ref, s_ref, kc_ref, vc_ref):
    t, dh = q_ref.shape
    blk = MOBA_BLOCK
    nb = t // blk

    kc_ref[:, 0:dh] = k_ref[...]
    kc_ref[:, dh:dh + BIAS_COLS] = kx_ref[...]
    vc_ref[:, 0:dh] = v_ref[...]
    vc_ref[:, dh:2 * dh] = jnp.ones((t, dh), bf16)

    for n in range(nb):
        km_ref[n:n + 1, :] = jnp.mean(k_ref[n * blk:(n + 1) * blk, :].astype(f32), axis=0, keepdims=True)
    gate_t = _dot_nt(km_ref[...], q_ref[...].astype(f32), precision=HIGHEST)

    blk_id = lax.broadcasted_iota(jnp.int32, (nb, blk), 0)
    ri = lax.broadcasted_iota(jnp.int32, (blk, blk), 0)
    ci = lax.broadcasted_iota(jnp.int32, (blk, blk), 1)
    causal = ri >= ci
    eye = (ri == ci).astype(bf16)
    lane = lax.broadcasted_iota(jnp.int32, (blk, BIAS_COLS), 1)
    rowf = lax.broadcasted_iota(jnp.int32, (blk, BIAS_COLS), 0).astype(f32)
    slope = sl_ref[:, 0:1]

    def logits_pass(j):
        qj = q_ref[j * blk:(j + 1) * blk, :]
        masked_t = jnp.zeros((nb, blk), f32)
        if j > 0:
            g = jnp.where(blk_id < j, gate_t[:, j * blk:(j + 1) * blk], NEG)
            for n in range(j):
                gn = g[n:n + 1, :]
                ahead = (g > gn) | ((g == gn) & (blk_id < n))
                rank = jnp.sum(jnp.where(ahead, 1.0, 0.0), axis=0, keepdims=True)
                masked_t = jnp.where((blk_id == n) & (rank >= float(MOBA_TOPK)), -1.0, masked_t)
        masked_pad = jnp.concatenate([masked_t, jnp.zeros((BIAS_COLS - nb, blk), f32)], axis=0).astype(bf16)
        qx = _dot_nt(eye, masked_pad)
        qx = jnp.where(lane == nb, -slope * float(blk * j), qx)
        qx = jnp.where(lane == nb + 1, -slope * rowf, qx)
        qx = jnp.where((lane == nb + 2) | (lane == nb + 3), 1.0, qx).astype(bf16)
        qc = jnp.concatenate([qj, qx], axis=1)
        mx = None
        for n in range(j + 1):
            ks = slice(n * blk, (n + 1) * blk)
            lg = _dot_nt(qc, kc_ref[ks, :])
            if n == j:
                lg = jnp.where(causal, lg, NEG)
            s_ref[j % 2, :, ks] = lg
            half = jnp.maximum(lg[:, 0:LANE], lg[:, LANE:2 * LANE])
            mx = half if mx is None else jnp.maximum(mx, half)
        return jnp.max(mx, axis=-1, keepdims=True)

    def value_pass(j, m):
        acc = jnp.zeros((blk, 2 * dh), f32)
        for n in range(j + 1):
            ks = slice(n * blk, (n + 1) * blk)
            p = jnp.exp((s_ref[j % 2, :, ks] - m).astype(bf16))
            acc = acc + jnp.dot(p, vc_ref[ks, :], preferred_element_type=f32)
        o = acc[:, 0:dh] / acc[:, dh:2 * dh]
        o = o * lax.rsqrt(jnp.mean(o * o, axis=-1, keepdims=True) + EPS) * ng_ref[...]
        y_ref[j * blk:(j + 1) * blk, :] = o.astype(y_ref.dtype)

    m_prev = logits_pass(0)
    for j in range(1, nb):
        m_cur = logits_pass(j)
        value_pass(j - 1, m_prev)
        m_prev = m_cur
    value_pass(nb - 1, m_prev)


def _moba(z, col0, norm_g):
    b, t, _ = z.shape
    dh = HEAD_DIM
    nh = norm_g.shape[0] // dh
    blk = MOBA_BLOCK
    assert t % blk == 0
    nb = t // blk
    assert nb + 4 <= BIAS_COLS
    slopes = 2.0 ** (-(8.0 / nh) * jnp.arange(1, nh + 1, dtype=f32))
    pos = jnp.arange(t, dtype=jnp.int32)
    col = jnp.arange(BIAS_COLS, dtype=jnp.int32)[None, :]
    kblk = (pos // blk)[:, None]
    base = jnp.where(col == kblk, MASK_BIG, 0.0) + jnp.where((col == nb) | (col == nb + 1), 1.0, 0.0)
    per_head = (jnp.where(col == nb + 2, (kblk * blk).astype(f32), 0.0)
                + jnp.where(col == nb + 3, (pos % blk).astype(f32)[:, None], 0.0))
    kx = (base[None] + slopes[:, None, None] * per_head[None]).astype(bf16)
    slope_rows = jnp.broadcast_to(slopes[:, None, None], (nh, 1, LANE))

    def zspec(j):
        return pl.BlockSpec((None, t, dh), lambda bi, h: (bi, 0, col0 + j * nh + h))

    return pl.pallas_call(
        _moba_kernel,
        grid=(b, nh),
        in_specs=[
            zspec(0), zspec(1), zspec(2),
            pl.BlockSpec((None, t, BIAS_COLS), lambda bi, h: (h, 0, 0)),
            pl.BlockSpec((None, 1, LANE), lambda bi, h: (h, 0, 0)),
            pl.BlockSpec((None, 1, dh), lambda bi, h: (h, 0, 0)),
        ],
        out_specs=pl.BlockSpec((None, t, dh), lambda bi, h: (bi, 0, h)),
        out_shape=jax.ShapeDtypeStruct((b, t, nh * dh), bf16),
        scratch_shapes=[pltpu.VMEM((nb, dh), f32), pltpu.VMEM((2, blk, t), f32), pltpu.VMEM((t, dh + BIAS_COLS), bf16),
                        pltpu.VMEM((t, 2 * dh), bf16)],
        compiler_params=_cparams("parallel", "parallel"),
        name="moba_attention",
    )(z, z, z, kx, slope_rows, norm_g.reshape(nh, 1, dh))


def _router_logits_t(w_t32, h32, hb):
    h_lo = (h32 - hb.astype(f32)).astype(bf16)
    w_hi = w_t32.astype(bf16)
    w_lo = (w_t32 - w_hi.astype(f32)).astype(bf16)
    return _dot_nt(w_hi, hb) + _dot_nt(w_hi, h_lo) + _dot_nt(w_lo, hb)


def _top2_t(logits):
    ne = logits.shape[0]
    eid = lax.broadcasted_iota(jnp.int32, logits.shape, 0)
    m1 = jnp.max(logits, axis=0, keepdims=True)
    i1 = jnp.min(jnp.where(logits == m1, eid, ne), axis=0, keepdims=True)
    rest = jnp.where(eid == i1, -jnp.inf, logits)
    m2 = jnp.max(rest, axis=0, keepdims=True)
    i2 = jnp.min(jnp.where(rest == m2, eid, ne), axis=0, keepdims=True)
    e2 = jnp.exp(m2 - m1)
    p1 = 1.0 / (1.0 + e2)
    return i1, i2, p1, e2 * p1


OUTPROJ_SPLIT = 2


def _outproj_kernel(yc_ref, ym_ref, ya_ref, w_ref, x_ref, gt_ref, g_ref, sh_ref, sc_ref, *rest, routed):
    if routed:
        wr_ref, br_ref, xo_ref, h_ref, idx_ref, p_ref = rest
    else:
        xo_ref, h_ref = rest
    tm = x_ref.shape[0]
    sub = tm // OUTPROJ_SPLIT
    blocks = [slice(r * sub, (r + 1) * sub) for r in range(OUTPROJ_SPLIT)]
    accs = [jnp.dot(jnp.concatenate([yc_ref[rs, :], ym_ref[rs, :], ya_ref[rs, :]], axis=1), w_ref[...],
                    preferred_element_type=f32) for rs in blocks]
    for rs, acc in zip(blocks, accs):
        xn = x_ref[rs, :] + gt_ref[...] * acc
        xo_ref[rs, :] = xn
        h = _norm_modulate(xn, g_ref[...], sh_ref[...], sc_ref[...])
        hb = h.astype(bf16)
        h_ref[rs, :] = h if routed else hb
        if routed:
            i1, i2, p1, p2 = _top2_t(_router_logits_t(wr_ref[...], h, hb) + br_ref[...])
            first = lax.broadcasted_iota(jnp.int32, (TOP_K, sub), 0) == 0
            idx_ref[:, rs] = jnp.where(first, i1, i2)
            p_ref[:, rs] = jnp.where(first, p1, p2)


def _out_projection(y_conv, y_ml, y_att, w_out, x, mod4, layer, g_ffn, router=None, tm=512):
    b, t, d = x.shape
    c1, c2, c3 = y_conv.shape[-1], y_ml.shape[-1], y_att.shape[-1]
    assert c1 + c2 + c3 == w_out.shape[0]
    row = lambda c: pl.BlockSpec((None, tm, c), lambda bi, i: (bi, i, 0))
    modspec = lambda chunk: pl.BlockSpec((None, None, 1, d), lambda bi, i: (layer, bi, 0, chunk))
    in_specs = [
        row(c1), row(c2), row(c3),
        pl.BlockSpec(w_out.shape, lambda bi, i: (0, 0)),
        row(d), modspec(2),
        pl.BlockSpec((1, d), lambda bi, i: (0, 0)), modspec(3), modspec(4),
    ]
    args = [y_conv, y_ml, y_att, w_out, x, mod4, g_ffn.reshape(1, d), mod4, mod4]
    out_specs = [row(d), row(d)]
    out_shape = [jax.ShapeDtypeStruct((b, t, d), f32), jax.ShapeDtypeStruct((b, t, d), bf16 if router is None else f32)]
    if router is not None:
        w_r, b_r = router
        ne = w_r.shape[1]
        in_specs += [pl.BlockSpec((ne, d), lambda bi, i: (0, 0)), pl.BlockSpec((ne, 1), lambda bi, i: (0, 0))]
        args += [w_r.T, b_r.reshape(ne, 1)]
        kspec = pl.BlockSpec((None, TOP_K, tm), lambda bi, i: (bi, 0, i))
        out_specs += [kspec, kspec]
        out_shape += [jax.ShapeDtypeStruct((b, TOP_K, t), jnp.int32), jax.ShapeDtypeStruct((b, TOP_K, t), f32)]
    return pl.pallas_call(
        functools.partial(_outproj_kernel, routed=router is not None),
        grid=(b, t // tm),
        in_specs=in_specs,
        out_specs=out_specs,
        out_shape=out_shape,
        compiler_params=_cparams("parallel", "parallel"),
        name="out_projection",
    )(*args)


def _swiglu_rows(hb, wg_ref, wu_ref, wd_ref):
    g = jnp.dot(hb, wg_ref[...].astype(bf16), preferred_element_type=f32)
    u = jnp.dot(hb, wu_ref[...].astype(bf16), preferred_element_type=f32)
    a = (_silu(g) * u).astype(bf16)
    return jnp.dot(a, wd_ref[...].astype(bf16), preferred_element_type=f32)


def _dense_swiglu_kernel(h_ref, wg_ref, wu_ref, wd_ref, o_ref, acc_ref):
    f = pl.program_id(1)

    @pl.when(f == 0)
    def _():
        acc_ref[...] = jnp.zeros_like(acc_ref)

    acc_ref[...] += _swiglu_rows(h_ref[...], wg_ref, wu_ref, wd_ref)

    @pl.when(f == pl.num_programs(1) - 1)
    def _():
        o_ref[...] = acc_ref[...].astype(o_ref.dtype)


def _dense_swiglu(h, wg, wu, wd, layer, tm=1024, tf=512):
    n, d = h.shape
    ff = wg.shape[-1]
    return pl.pallas_call(
        _dense_swiglu_kernel,
        grid=(n // tm, ff // tf),
        in_specs=[
            pl.BlockSpec((tm, d), lambda i, f: (i, 0)),
            pl.BlockSpec((None, d, tf), lambda i, f: (layer, 0, f)),
            pl.BlockSpec((None, d, tf), lambda i, f: (layer, 0, f)),
            pl.BlockSpec((None, tf, d), lambda i, f: (layer, f, 0)),
        ],
        out_specs=pl.BlockSpec((tm, d), lambda i, f: (i, 0)),
        out_shape=jax.ShapeDtypeStruct((n, d), bf16),
        scratch_shapes=[pltpu.VMEM((tm, d), f32)],
        compiler_params=_cparams("parallel", "arbitrary"),
        name="dense_swiglu",
    )(h, wg, wu, wd)


ROW_UNIT = 128
GATHER_UNROLL = 8


def _moe_swiglu_kernel(te_ref, nq_ref, tok_ref, tok_next_ref, h_hbm, wg_ref, wu_ref, wd_ref, o_ref,
                       hrows_ref, hb_ref, acc_ref, sem):
    i = pl.program_id(0)
    f = pl.program_id(1)
    nt = pl.num_programs(0)
    nq = nq_ref[i]
    tm = acc_ref.shape[0]

    def row_copy(src_row, dst_row, rows=1):
        return pltpu.make_async_copy(h_hbm.at[pl.ds(src_row, rows)], hrows_ref.at[pl.ds(dst_row, rows)], sem)

    def start_gather(tokens_ref, units):
        def issue(g, _):
            for u in range(GATHER_UNROLL):
                r = g * GATHER_UNROLL + u
                row_copy(tokens_ref[0, r], r).start()
            return 0

        lax.fori_loop(0, units * (ROW_UNIT // GATHER_UNROLL), issue, 0)

    @pl.when(f == 0)
    def _():
        acc_ref[...] = jnp.zeros_like(acc_ref)

        @pl.when((i == 0) & (nq > 0))
        def _():
            start_gather(tok_ref, nq)

        @pl.when(nq > 0)
        def _():
            row_copy(0, 0, nq * ROW_UNIT).wait()
            for q in range(tm // ROW_UNIT):
                @pl.when(q < nq)
                def _(q=q):
                    qs = slice(q * ROW_UNIT, (q + 1) * ROW_UNIT)
                    hb_ref[qs, :] = hrows_ref[qs, :].astype(bf16)

        nq_next = nq_ref[jnp.minimum(i + 1, nt - 1)]

        @pl.when((i + 1 < nt) & (nq_next > 0))
        def _():
            start_gather(tok_next_ref, nq_next)

    for q in range(1, tm // ROW_UNIT + 1):
        @pl.when(nq == q)
        def _(q=q):
            rows = q * ROW_UNIT
            acc_ref[0:rows, :] += _swiglu_rows(hb_ref[0:rows, :], wg_ref, wu_ref, wd_ref)

    @pl.when(f == pl.num_programs(1) - 1)
    def _():
        o_ref[...] = acc_ref[...].astype(o_ref.dtype)


def _moe_swiglu(h32, row_token, tile_expert, tile_units, wg, wu, wd, tm=1024, tf=256):
    n, d = h32.shape
    nt = row_token.shape[0]
    ff = wg.shape[-1]
    nf = ff // tf
    assert tm % ROW_UNIT == 0

    def wmap(i, f, te, nq):
        return (te[i], 0, jnp.where(nq[i] > 0, f, nf - 1))

    def wdmap(i, f, te, nq):
        return (te[i], jnp.where(nq[i] > 0, f, nf - 1), 0)

    grid_spec = pltpu.PrefetchScalarGridSpec(
        num_scalar_prefetch=2,
        grid=(nt, nf),
        in_specs=[
            pl.BlockSpec((None, 1, tm), lambda i, f, te, nq: (i, 0, 0), memory_space=pltpu.SMEM),
            pl.BlockSpec((None, 1, tm), lambda i, f, te, nq: (jnp.minimum(i + 1, nt - 1), 0, 0),
                         memory_space=pltpu.SMEM),
            pl.BlockSpec(memory_space=pl.ANY),
            pl.BlockSpec((None, d, tf), wmap),
            pl.BlockSpec((None, d, tf), wmap),
            pl.BlockSpec((None, tf, d), wdmap),
        ],
        out_specs=pl.BlockSpec((tm, d), lambda i, f, te, nq: (i, 0)),
        scratch_shapes=[
            pltpu.VMEM((tm, d), f32),
            pltpu.VMEM((tm, d), bf16),
            pltpu.VMEM((tm, d), f32),
            pltpu.SemaphoreType.DMA,
        ],
    )
    return pl.pallas_call(
        _moe_swiglu_kernel,
        grid_spec=grid_spec,
        out_shape=jax.ShapeDtypeStruct((nt * tm, d), bf16),
        compiler_params=_cparams("arbitrary", "arbitrary"),
        name="moe_swiglu",
    )(tile_expert, tile_units, row_token, row_token, h32, wg, wu, wd)


def _combine_kernel(*refs, n_y, weighted, last):
    x_ref, gt_ref = refs[0], refs[1]
    ys = [r[...].astype(f32) for r in refs[2:2 + n_y]]
    rest = refs[2 + n_y:]
    if weighted:
        p = rest[0][...]
        rest = rest[1:]
        ys = [p[:, k:k + 1] * y for k, y in enumerate(ys)]
    y = ys[0]
    for yk in ys[1:]:
        y = y + yk
    xn = x_ref[...] + gt_ref[...] * y
    if last:
        g_ref, o_ref = rest
        o_ref[...] = xn * lax.rsqrt(jnp.mean(xn * xn, axis=-1, keepdims=True) + EPS) * g_ref[...]
    else:
        g_ref, sh_ref, sc_ref, xo_ref, h_ref = rest
        xo_ref[...] = xn
        h_ref[...] = _norm_modulate(xn, g_ref[...], sh_ref[...], sc_ref[...]).astype(h_ref.dtype)


def _combine(x, mod4, layer, ys, probs, g_next, last, tm=512):
    b, t, d = x.shape
    spec = pl.BlockSpec((None, tm, d), lambda bi, i: (bi, i, 0))
    modspec = lambda lyr, chunk: pl.BlockSpec((None, None, 1, d), lambda bi, i: (lyr, bi, 0, chunk))
    in_specs = [spec, modspec(layer, 5)] + [spec] * len(ys)
    args = [x, mod4, *ys]
    if probs is not None:
        in_specs.append(pl.BlockSpec((None, tm, len(ys)), lambda bi, i: (bi, i, 0)))
        args.append(probs)
    in_specs.append(pl.BlockSpec((1, d), lambda bi, i: (0, 0)))
    args.append(g_next.reshape(1, d))
    if last:
        out_specs, out_shape = spec, jax.ShapeDtypeStruct((b, t, d), f32)
    else:
        in_specs += [modspec(layer + 1, 0), modspec(layer + 1, 1)]
        args += [mod4, mod4]
        out_specs = [spec, spec]
        out_shape = [jax.ShapeDtypeStruct((b, t, d), f32), jax.ShapeDtypeStruct((b, t, d), bf16)]
    return pl.pallas_call(
        functools.partial(_combine_kernel, n_y=len(ys), weighted=probs is not None, last=last),
        grid=(b, t // tm),
        in_specs=in_specs,
        out_specs=out_specs,
        out_shape=out_shape,
        compiler_params=_cparams("parallel", "parallel"),
        name="ffn_combine",
    )(*args)


def _routing_tables(idx, n_experts, tm):
    n = idx.shape[0]
    na = n * TOP_K
    nt = na // tm + n_experts
    e_flat = idx.reshape(na)
    onehot = (e_flat[:, None] == jnp.arange(n_experts, dtype=jnp.int32)[None, :]).astype(jnp.int32)
    counts = jnp.sum(onehot, axis=0)
    rank = jnp.sum((jnp.cumsum(onehot, axis=0) - 1) * onehot, axis=1)
    tiles_per = (counts + tm - 1) // tm
    tile_end = jnp.cumsum(tiles_per)
    tile_start = tile_end - tiles_per
    slot = tile_start[e_flat] * tm + rank
    n_valid = tile_end[-1:].astype(jnp.int32)
    tile_ids = jnp.arange(nt, dtype=jnp.int32)
    tile_expert = jnp.sum((tile_ids[:, None] >= tile_end[None, :]).astype(jnp.int32), axis=1)
    last_expert = jnp.sum((n_valid - 1 >= tile_end).astype(jnp.int32))
    tile_expert = jnp.where(tile_ids < n_valid, tile_expert, last_expert).astype(jnp.int32)
    tile_rows = jnp.clip(counts[tile_expert] - (tile_ids - tile_start[tile_expert]) * tm, 0, tm)
    tile_units = jnp.where(tile_ids < n_valid, (tile_rows + ROW_UNIT - 1) // ROW_UNIT, 0).astype(jnp.int32)
    token = jnp.arange(na, dtype=jnp.int32) // TOP_K
    row_token = jnp.zeros((nt * tm,), jnp.int32).at[slot].set(token)
    return slot.reshape(n, TOP_K), row_token.reshape(nt, 1, tm), tile_expert, tile_units


def kernel(x, c, w_mod, b_mod, g_mix, g_ffn, w_in, conv_w, conv_b, conv_ln_g, conv_ln_b, ml_conv_w, ml_conv_b, ml_b_i, ml_b_f, ml_norm_g, attn_norm_g, w_out, ffn_w_gate, ffn_w_up, ffn_w_down, moe_w_router, moe_b_router, moe_w_gate, moe_w_up, moe_w_down, g_final):
    b, t, d = x.shape
    depth = w_mod.shape[0]
    conv_ch = conv_w.shape[-1]
    ml_dim = ml_norm_g.shape[-1]
    ml_heads = ml_b_i.shape[-1]
    n_experts = moe_w_router.shape[-1]
    att_dim = attn_norm_g.shape[-1]
    n_tok = b * t
    tm_ffn = 1024

    mod = _adaln_mod(c, w_mod, b_mod)
    mod4 = mod.reshape(depth, b, 1, 6 * d)

    gate0 = 2 * conv_ch + 4 * ml_dim
    gate1 = gate0 + 2 * ml_heads
    w_in_t = jnp.swapaxes(w_in, 1, 2)

    h = _prep(x, g_mix[0], mod4, 0, 0)
    out = None
    for l in range(depth):
        z, zg = _in_projection(h.reshape(n_tok, d), w_in_t, l, gate0, gate1, att_dim, HEAD_DIM ** -0.5)
        z = z.reshape(b, t, -1)
        zg = zg.reshape(b, t, -1)
        zg_t = jnp.swapaxes(zg, 1, 2)

        y_conv = _conv_module(z, conv_w[l], conv_b[l], conv_ln_g[l], conv_ln_b[l])
        y_ml = _mlstm(z, zg_t, (2 * conv_ch) // ml_dim, ml_conv_w[l], ml_conv_b[l], ml_b_i[l], ml_b_f[l], ml_norm_g[l])
        y_att = _moba(z, gate0 // HEAD_DIM, attn_norm_g[l])

        j = l // 2
        w_o = w_out[l].astype(bf16)
        if l % 2 == 0:
            x, h = _out_projection(y_conv, y_ml, y_att, w_o, x, mod4, l, g_ffn[l])
            y = _dense_swiglu(h.reshape(n_tok, d), ffn_w_gate, ffn_w_up, ffn_w_down, j, tm=tm_ffn)
            ys, y_probs = [y.reshape(b, t, d)], None
        else:
            x, h32, idx, probs = _out_projection(y_conv, y_ml, y_att, w_o, x, mod4, l, g_ffn[l],
                                                 router=(moe_w_router[j], moe_b_router[j]))
            slot, row_token, tile_expert, tile_units = _routing_tables(
                jnp.swapaxes(idx, 1, 2).reshape(n_tok, TOP_K), n_experts, tm_ffn)
            ff = moe_w_gate.shape[-1]
            y_sorted = _moe_swiglu(
                h32.reshape(n_tok, d), row_token, tile_expert + j * n_experts, tile_units,
                moe_w_gate.reshape(-1, d, ff), moe_w_up.reshape(-1, d, ff), moe_w_down.reshape(-1, ff, d), tm=tm_ffn)
            ys = [y_sorted.at[slot[:, k]].get(mode="promise_in_bounds").reshape(b, t, d) for k in range(TOP_K)]
            y_probs = jnp.swapaxes(probs, 1, 2)

        if l + 1 < depth:
            x, h = _combine(x, mod4, l, ys, y_probs, g_mix[l + 1], last=False)
        else:
            out = _combine(x, mod4, l, ys, y_probs, g_final, last=True)
    return out
```
